```python
import jax, jax.numpy as jnp
from jax import lax
import numpy as np

D_MODEL = 1024
BATCH = 8
SEQ = 2048
DEPTH = 1
DEC_BATCH = 128
DEC_SEQ = 4
PAST_LEN = 16384
PAGE_SIZE = 128

HEAD_SIZE = 64
D_RWKV = D_MODEL
N_HEADS = D_RWKV // HEAD_SIZE
D_DECAY_LORA = 64
D_AAA_LORA = 64
D_GATE_LORA = 160
D_CONV = D_MODEL
CONV_WIDTH = 31
N_EXPERTS = 64
TOP_K = 6
D_EXPERT = 256
D_SHARED = 256
ROUTED_SCALE = 2.5
RMS_EPS = 1e-6
LN_EPS = 1e-5
GN_EPS = HEAD_SIZE * 1e-5
N_SHIFT = 3 * D_RWKV + D_DECAY_LORA + D_AAA_LORA + D_GATE_LORA
N_IN = N_SHIFT + 2 * D_CONV + 2 * D_MODEL

kernel_name = "rwkv7_conformer_moe_hybrid_step"


def rmsnorm(x, g):
    xf = x.astype(jnp.float32)
    y = xf * lax.rsqrt(jnp.mean(xf * xf, axis=-1, keepdims=True) + RMS_EPS)
    return (y * g.astype(jnp.float32)).astype(x.dtype)


def layernorm(x, g, b, eps):
    xf = x.astype(jnp.float32)
    mu = jnp.mean(xf, axis=-1, keepdims=True)
    var = jnp.mean(jnp.square(xf - mu), axis=-1, keepdims=True)
    y = (xf - mu) * lax.rsqrt(var + eps)
    return y * g.astype(jnp.float32) + b.astype(jnp.float32)


def wkv7_scan(S0, r, w, k, v, kk, a):
    tmajor = lambda t: jnp.moveaxis(t.astype(jnp.float32), 1, 0)

    def step(S, inp):
        rt, wt, kt, vt, kkt, at = inp
        sa = jnp.einsum('bhvk,bhk->bhv', S, -kkt)
        S = (S * wt[:, :, None, :] + sa[..., None] * (kkt * at)[:, :, None, :]
             + vt[..., None] * kt[:, :, None, :])
        return S, jnp.einsum('bhvk,bhk->bhv', S, rt)

    S, y = lax.scan(step, S0.astype(jnp.float32),
                    (tmajor(r), tmajor(w), tmajor(k), tmajor(v), tmajor(kk), tmajor(a)))
    return S, jnp.moveaxis(y, 0, 1)


def hybrid_layer(x, c, wkv0, shift0, conv0, p):
    B, T, _ = x.shape
    dt = x.dtype
    mod = (jax.nn.silu(c) @ p['w_ada'] + p['b_ada'])[:, None, :]
    sh_m, sc_m, gt_m, sh_f, sc_f, gt_f = jnp.split(mod, 6, axis=-1)

    h = rmsnorm(x, p['mix_pre_g']) * (1 + sc_m) + sh_m
    proj = h @ p['w_in']
    p_rw, p_cv, p_gt = jnp.split(proj, [N_SHIFT, N_SHIFT + 2 * D_CONV], axis=-1)

    prev = jnp.concatenate([shift0[:, None, :].astype(dt), p_rw[:, :-1]], axis=1)
    xs = p_rw + (prev - p_rw) * p['mu_shift']
    r, k, v, xw, xa, xg = jnp.split(
        xs, [D_RWKV, 2 * D_RWKV, 3 * D_RWKV, 3 * D_RWKV + D_DECAY_LORA,
             3 * D_RWKV + D_DECAY_LORA + D_AAA_LORA], axis=-1)
    w_log = -jax.nn.softplus(-(p['w0'] + jnp.tanh(xw) @ p['w_decay2'])) - 0.5
    decay = jnp.exp(-jnp.exp(w_log.astype(jnp.float32)))
    a = jax.nn.sigmoid(p['a0'] + xa @ p['w_aaa2'])
    g = jax.nn.sigmoid(xg) @ p['w_gate2']
    heads = lambda t: t.reshape(B, T, N_HEADS, HEAD_SIZE)
    kk = heads((k * p['k_k']).astype(jnp.float32))
    kk = kk / jnp.maximum(jnp.linalg.norm(kk, axis=-1, keepdims=True), 1e-12)
    k = k * (1 + (a - 1) * p['k_a'])
    rh, kh, vh = heads(r), heads(k), heads(v)
    wkv, y = wkv7_scan(wkv0, rh, heads(decay), kh, vh, kk, heads(a))
    mu = jnp.mean(y, axis=-1, keepdims=True)
    var = jnp.mean(jnp.square(y - mu), axis=-1, keepdims=True)
    y = ((y - mu) * lax.rsqrt(var + GN_EPS)).reshape(B, T, D_RWKV)
    y = y * p['gn_g'].astype(jnp.float32) + p['gn_b'].astype(jnp.float32)
    bonus = jnp.sum(rh.astype(jnp.float32) * kh.astype(jnp.float32) * p['r_k'].astype(jnp.float32),
                    axis=-1, keepdims=True) * vh.astype(jnp.float32)
    y = y + bonus.reshape(B, T, D_RWKV)
    y_a = (y.astype(dt) * g) @ p['w_o_rwkv']

    glu = p_cv[..., :D_CONV] * jax.nn.sigmoid(p_cv[..., D_CONV:])
    padded = jnp.concatenate([conv0.astype(dt), glu], axis=1)
    dw = lax.conv_general_dilated(
        padded, p['dw_w'][:, None, :], window_strides=(1,), padding='VALID',
        dimension_numbers=('NWC', 'WIO', 'NWC'), feature_group_count=D_CONV) + p['dw_b']
    new_conv = padded[:, -(CONV_WIDTH - 1):]
    u = jax.nn.silu(layernorm(dw, p['conv_ln_g'], p['conv_ln_b'], LN_EPS)).astype(dt)
    y_b = u @ p['w_conv_out']

    g_a, g_b = jnp.split(p_gt, 2, axis=-1)
    merged = jax.nn.sigmoid(g_a) * y_a + jax.nn.sigmoid(g_b) * y_b
    x = x + gt_m * rmsnorm(merged @ p['w_out'], p['mix_post_g'])

    hf = rmsnorm(x, p['ffn_pre_g']) * (1 + sc_f) + sh_f
    t = hf.reshape(B * T, D_MODEL)
    scores = jax.nn.sigmoid((t @ p['w_router']).astype(jnp.float32))
    _, idx = lax.top_k(scores + p['router_bias'].astype(jnp.float32), TOP_K)
    sel = jnp.take_along_axis(scores, idx, axis=-1)
    wts = sel / jnp.sum(sel, axis=-1, keepdims=True) * ROUTED_SCALE
    gates = jnp.einsum('tk,tke->te', wts, jax.nn.one_hot(idx, N_EXPERTS, dtype=jnp.float32)).astype(dt)
    out = (jax.nn.silu(t @ p['w_sh_gate']) * (t @ p['w_sh_up'])) @ p['w_sh_down']
    for e in range(N_EXPERTS):
        he = jax.nn.silu(t @ p['w_exp_gate'][e]) * (t @ p['w_exp_up'][e])
        out = out + gates[:, e:e + 1] * (he @ p['w_exp_down'][e])
    x = x + gt_f * rmsnorm(out.reshape(B, T, D_MODEL), p['ffn_post_g'])
    return (x, wkv.astype(wkv0.dtype), p_rw[:, -1].astype(shift0.dtype),
            new_conv.astype(conv0.dtype))


def setup_inputs(seed: int = 0) -> dict:
    key = jax.random.key(seed)
    ks = iter(jax.random.split(key, 48))
    nrm = lambda shape, s: jax.random.normal(next(ks), shape, jnp.float32) * s
    gain = lambda n: 1.0 + nrm((n,), 0.05)
    return {
        'x_prompt': nrm((BATCH, SEQ, D_MODEL), 1.0),
        'x_sample': nrm((DEC_BATCH, DEC_SEQ, D_MODEL), 1.0),
        'state_wkv': nrm((DEC_BATCH, N_HEADS, HEAD_SIZE, HEAD_SIZE), 1.0),
        'state_shift': nrm((DEC_BATCH, N_SHIFT), 1.0),
        'state_conv': nrm((DEC_BATCH, CONV_WIDTH - 1, D_CONV), 0.5),
        'c_prompt': nrm((BATCH, D_MODEL), 1.0),
        'c_sample': nrm((DEC_BATCH, D_MODEL), 1.0),
        'w_ada': nrm((D_MODEL, 6 * D_MODEL), 0.5 * D_MODEL ** -0.5),
        'b_ada': nrm((6 * D_MODEL,), 0.02),
        'mix_pre_g': gain(D_MODEL),
        'mix_post_g': gain(D_MODEL),
        'w_in': nrm((D_MODEL, N_IN), D_MODEL ** -0.5),
        'mu_shift': jax.random.uniform(next(ks), (N_SHIFT,), jnp.float32),
        'w0': jax.random.uniform(next(ks), (D_RWKV,), jnp.float32, minval=-6.5, maxval=-1.5),
        'w_decay2': nrm((D_DECAY_LORA, D_RWKV), 0.1 * D_DECAY_LORA ** -0.5),
        'a0': nrm((D_RWKV,), 0.3),
        'w_aaa2': nrm((D_AAA_LORA, D_RWKV), D_AAA_LORA ** -0.5),
        'w_gate2': nrm((D_GATE_LORA, D_RWKV), D_GATE_LORA ** -0.5),
        'k_k': 0.85 + nrm((D_RWKV,), 0.05),
        'k_a': gain(D_RWKV),
        'r_k': nrm((N_HEADS, HEAD_SIZE), 0.1),
        'gn_g': gain(D_RWKV),
        'gn_b': nrm((D_RWKV,), 0.02),
        'w_o_rwkv': nrm((D_RWKV, D_MODEL), D_RWKV ** -0.5),
        'dw_w': nrm((CONV_WIDTH, D_CONV), CONV_WIDTH ** -0.5),
        'dw_b': nrm((D_CONV,), 0.02),
        'conv_ln_g': gain(D_CONV),
        'conv_ln_b': nrm((D_CONV,), 0.02),
        'w_conv_out': nrm((D_CONV, D_MODEL), D_CONV ** -0.5),
        'w_out': nrm((D_MODEL, D_MODEL), D_MODEL ** -0.5),
        'ffn_pre_g': gain(D_MODEL),
        'ffn_post_g': gain(D_MODEL),
        'w_router': nrm((D_MODEL, N_EXPERTS), D_MODEL ** -0.5),
        'router_bias': nrm((N_EXPERTS,), 0.01),
        'w_exp_gate': nrm((N_EXPERTS, D_MODEL, D_EXPERT), D_MODEL ** -0.5),
        'w_exp_up': nrm((N_EXPERTS, D_MODEL, D_EXPERT), D_MODEL ** -0.5),
        'w_exp_down': nrm((N_EXPERTS, D_EXPERT, D_MODEL), D_EXPERT ** -0.5),
        'w_sh_gate': nrm((D_MODEL, D_SHARED), D_MODEL ** -0.5),
        'w_sh_up': nrm((D_MODEL, D_SHARED), D_MODEL ** -0.5),
        'w_sh_down': nrm((D_SHARED, D_MODEL), D_SHARED ** -0.5),
    }


def reference(x_prompt, x_sample, state_wkv, state_shift, state_conv, c_prompt, c_sample,
              w_ada, b_ada, mix_pre_g, mix_post_g, w_in, mu_shift, w0, w_decay2, a0, w_aaa2,
              w_gate2, k_k, k_a, r_k, gn_g, gn_b, w_o_rwkv, dw_w, dw_b, conv_ln_g, conv_ln_b,
              w_conv_out, w_out, ffn_pre_g, ffn_post_g, w_router, router_bias, w_exp_gate,
              w_exp_up, w_exp_down, w_sh_gate, w_sh_up, w_sh_down):
    p = {'w_ada': w_ada, 'b_ada': b_ada, 'mix_pre_g': mix_pre_g, 'mix_post_g': mix_post_g,
         'w_in': w_in, 'mu_shift': mu_shift, 'w0': w0, 'w_decay2': w_decay2, 'a0': a0,
         'w_aaa2': w_aaa2, 'w_gate2': w_gate2, 'k_k': k_k, 'k_a': k_a, 'r_k': r_k,
         'gn_g': gn_g, 'gn_b': gn_b, 'w_o_rwkv': w_o_rwkv, 'dw_w': dw_w, 'dw_b': dw_b,
         'conv_ln_g': conv_ln_g, 'conv_ln_b': conv_ln_b, 'w_conv_out': w_conv_out,
         'w_out': w_out, 'ffn_pre_g': ffn_pre_g, 'ffn_post_g': ffn_post_g,
         'w_router': w_router, 'router_bias': router_bias, 'w_exp_gate': w_exp_gate,
         'w_exp_up': w_exp_up, 'w_exp_down': w_exp_down, 'w_sh_gate': w_sh_gate,
         'w_sh_up': w_sh_up, 'w_sh_down': w_sh_down}
    Bp = x_prompt.shape[0]
    dt = x_prompt.dtype
    yp, sp = x_prompt, x_sample
    for _ in range(DEPTH):
        yp, wkv_p, shift_p, conv_p = hybrid_layer(
            yp, c_prompt,
            jnp.zeros((Bp, N_HEADS, HEAD_SIZE, HEAD_SIZE), dt),
            jnp.zeros((Bp, N_SHIFT), dt),
            jnp.zeros((Bp, CONV_WIDTH - 1, D_CONV), dt), p)
        sp, wkv_s, shift_s, conv_s = hybrid_layer(
            sp, c_sample, state_wkv, state_shift, state_conv, p)
    return (yp, sp, wkv_p, shift_p, conv_p, wkv_s, shift_s, conv_s)
```

```python
import functools

import jax
import jax.numpy as jnp
from jax import lax
from jax.experimental import pallas as pl
from jax.experimental.pallas import tpu as pltpu

F32 = jnp.float32
BF16 = jnp.bfloat16

D = 1024
H = 16
N = 64
E = 64
DE = 256
TOP_K = 6
ROUTED_SCALE = 2.5
CONV_W = 31
RMS_EPS = 1e-6
LN_EPS = 1e-5
GN_EPS = N * 1e-5

C_R, C_K, C_V = 0, 1024, 2048
C_XW, C_XA, C_XG = 3072, 3200, 3328
NRW = 3584
C_CVA, C_CVB, C_GA, C_GB = 3584, 4608, 5632, 6656
NIN = 7680

LANES = 128
HG = 4
GL = HG * N
NG = H // HG
WKV_SEQS = 8
VMEM_LIMIT = 56 * 1024 * 1024


def _dot(a, b):
    return jnp.dot(a, b, preferred_element_type=F32)


def _sigmoid(x):
    return jax.nn.sigmoid(x)


def _split_bf16(x):
    hi = x.astype(BF16)
    lo = (x - hi.astype(F32)).astype(BF16)
    return hi, lo


def _head_sum(x, seg, exp):
    hi, lo = _split_bf16(x)
    s = _dot(hi, seg) + _dot(lo, seg)
    shi, slo = _split_bf16(s)
    return _dot(shi, exp) + _dot(slo, exp)


def _params(sem, vmem=VMEM_LIMIT):
    return pltpu.CompilerParams(dimension_semantics=sem, vmem_limit_bytes=vmem)


def _mod_body(c_ref, w_ref, b_ref, o_ref):
    c = c_ref[...]
    s = (c * _sigmoid(c)).astype(BF16)
    o_ref[...] = _dot(s, w_ref[...].astype(BF16)) + b_ref[...]


def _mod_call(c_all, w_ada, b_ada):
    rows = c_all.shape[0]
    tn = 512
    return pl.pallas_call(
        _mod_body,
        grid=(6 * D // tn,),
        in_specs=[pl.BlockSpec((rows, D), lambda j: (0, 0)),
                  pl.BlockSpec((D, tn), lambda j: (0, j)),
                  pl.BlockSpec((1, tn), lambda j: (0, j))],
        out_specs=pl.BlockSpec((rows, tn), lambda j: (0, j)),
        out_shape=jax.ShapeDtypeStruct((rows, 6 * D), F32),
        compiler_params=_params(("arbitrary",)),
        name="ada_mod",
    )(c_all, w_ada, b_ada.reshape(1, 6 * D))


def _mod_spec(per_token, tm, rows_per_seq):
    if per_token:
        return pl.BlockSpec((None, tm, D), lambda i, *_: (0, i, 0))
    return pl.BlockSpec((None, 1, D), lambda i, *_: (i * tm // rows_per_seq, 0, 0))


def _proj_body(x_ref, sc_ref, sh_ref, g_ref, w_ref, o_ref, h_scr):
    @pl.when(pl.program_id(1) == 0)
    def _():
        x = x_ref[...]
        y = x * lax.rsqrt(jnp.mean(x * x, axis=-1, keepdims=True) + RMS_EPS) * g_ref[...]
        h_scr[...] = (y * (1.0 + sc_ref[...]) + sh_ref[...]).astype(BF16)

    o_ref[...] = _dot(h_scr[...], w_ref[...])


def _proj_call(x, sc, sh, g, w_in_p, per_token, tm, rows_per_seq):
    ntok = x.shape[0]
    tn = 1280
    mspec = _mod_spec(per_token, tm, rows_per_seq)
    return pl.pallas_call(
        _proj_body,
        grid=(ntok // tm, NIN // tn),
        in_specs=[pl.BlockSpec((tm, D), lambda i, j: (i, 0)),
                  mspec, mspec,
                  pl.BlockSpec((1, D), lambda i, j: (0, 0)),
                  pl.BlockSpec((D, tn), lambda i, j: (0, j))],
        out_specs=pl.BlockSpec((tm, tn), lambda i, j: (i, j)),
        out_shape=jax.ShapeDtypeStruct((ntok, NIN), F32),
        scratch_shapes=[pltpu.VMEM((tm, D), BF16)],
        compiler_params=_params(("arbitrary", "arbitrary")),
        name="in_proj",
    )(x, sc, sh, g, w_in_p)


def _prep_body(proj_ref, sh0_ref, cv0_ref, mu_ref, w0_ref, a0_ref, kk_ref, ka_ref, rk_ref,
               wd2_ref, wa2_ref, wg2_ref, dww_ref, dwb_ref, lng_ref, lnb_ref, wco_ref,
               seg_ref, exp_ref,
               al_o, be_o, k_o, r_o, w_o, v_o, g_o, bo_o, sga_o, yb_o, cv_o,
               buf_scr, ext_scr, *, tm):
    ti = pl.program_id(1)

    @pl.when(ti == 0)
    def _():
        buf_scr[7:8, :] = sh0_ref[...]
        ext_scr[0:2, :] = jnp.zeros((2, D), F32)
        ext_scr[2:32, :] = cv0_ref[...]

    p = proj_ref[:, 0:NRW]
    buf_scr[8:8 + tm, :] = p
    prev = buf_scr[7:7 + tm, :]
    xs = p + (prev - p) * mu_ref[...]
    buf_scr[7:8, :] = p[tm - 1:tm, :]

    r = xs[:, C_R:C_R + D]
    k = xs[:, C_K:C_K + D]
    v = xs[:, C_V:C_V + D]
    xw = xs[:, C_XW:C_XW + 128]
    xa = xs[:, C_XA:C_XA + 128]
    xg = xs[:, C_XG:C_XG + 256]

    lw = _dot(jnp.tanh(xw).astype(BF16), wd2_ref[...])
    z = -(w0_ref[...] + lw)
    softplus = jnp.maximum(z, 0.0) + jnp.log(1.0 + jnp.exp(-jnp.abs(z)))
    decay = jnp.exp(-jnp.exp(-softplus - 0.5))
    a = _sigmoid(a0_ref[...] + _dot(xa.astype(BF16), wa2_ref[...]))
    g_o[...] = _dot(_sigmoid(xg).astype(BF16), wg2_ref[...])

    seg = seg_ref[...]
    exp = exp_ref[...]
    kk = k * kk_ref[...]
    kk = kk / jnp.maximum(jnp.sqrt(_head_sum(kk * kk, seg, exp)), 1e-12)
    k2 = k * (1.0 + (a - 1.0) * ka_ref[...])
    bo_o[...] = _head_sum(r * k2 * rk_ref[...], seg, exp) * v
    al_o[...] = -kk
    be_o[...] = kk * a
    k_o[...] = k2
    r_o[...] = r
    w_o[...] = decay
    v_o[...] = v

    glu = proj_ref[:, C_CVA:C_CVA + D] * _sigmoid(proj_ref[:, C_CVB:C_CVB + D])
    ext_scr[32:32 + tm, :] = glu
    acc = jnp.zeros((tm, D), F32) + dwb_ref[...]
    for j in range(CONV_W):
        acc = acc + ext_scr[2 + j:2 + j + tm, :] * dww_ref[j:j + 1, :]
    tail = ext_scr[tm:tm + 32, :]
    cv_o[...] = tail[2:32, :]
    ext_scr[0:32, :] = tail
    mean = jnp.mean(acc, axis=-1, keepdims=True)
    cen = acc - mean
    var = jnp.mean(cen * cen, axis=-1, keepdims=True)
    ln = cen * lax.rsqrt(var + LN_EPS) * lng_ref[...] + lnb_ref[...]
    u = (ln * _sigmoid(ln)).astype(BF16)
    yb = _dot(u, wco_ref[...])
    sga_o[...] = _sigmoid(proj_ref[:, C_GA:C_GA + D])
    yb_o[...] = _sigmoid(proj_ref[:, C_GB:C_GB + D]) * yb


def _prep_call(proj, shift0_p, conv0, pw, bn, t, tm):
    nt = t // tm
    row = lambda shape: pl.BlockSpec(shape, lambda b, i: (0,) * len(shape))
    tok = pl.BlockSpec((None, tm, D), lambda b, i: (b, i, 0))
    ins = [pl.BlockSpec((None, tm, NIN), lambda b, i: (b, i, 0)),
           pl.BlockSpec((None, 1, NRW), lambda b, i: (b, 0, 0)),
           pl.BlockSpec((None, CONV_W - 1, D), lambda b, i: (b, 0, 0)),
           row((1, NRW)), row((1, D)), row((1, D)), row((1, D)), row((1, D)), row((1, D)),
           row((128, D)), row((128, D)), row((256, D)),
           row((CONV_W, D)), row((1, D)), row((1, D)), row((1, D)), row((D, D)),
           row((D, LANES)), row((LANES, D))]
    outs = [tok] * 10 + [pl.BlockSpec((None, CONV_W - 1, D), lambda b, i: (b, 0, 0))]
    out_shape = [jax.ShapeDtypeStruct((bn, t, D), F32)] * 10 + [
        jax.ShapeDtypeStruct((bn, CONV_W - 1, D), F32)]
    return pl.pallas_call(
        functools.partial(_prep_body, tm=tm),
        grid=(bn, nt),
        in_specs=ins,
        out_specs=outs,
        out_shape=out_shape,
        scratch_shapes=[pltpu.VMEM((tm + 8, NRW), F32), pltpu.VMEM((tm + 32, D), F32)],
        compiler_params=_params(("arbitrary", "arbitrary")),
        name="rwkv_prep_conv",
    )(proj, shift0_p, conv0, pw["mu"], pw["w0"], pw["a0"], pw["k_k"], pw["k_a"], pw["r_k"],
      pw["wd2"], pw["wa2"], pw["wg2"], pw["dw_w"], pw["dw_b"], pw["ln_g"], pw["ln_b"],
      pw["w_conv_out"], pw["seg"], pw["exp"])


def _wkv_body(al_ref, be_ref, k_ref, r_ref, w_ref, vh_ref, s0_ref, y_ref, so_ref, s_scr, *, tb):
    ti = pl.program_id(1)

    @pl.when(ti == 0)
    def _():
        s_scr[...] = s0_ref[...]

    row_x = lax.broadcasted_iota(jnp.int32, (8, GL), 0)
    lane_x = lax.broadcasted_iota(jnp.int32, (8, GL), 1)
    head_mask = (lane_x // N == row_x % HG).astype(F32)
    low_x = row_x < HG
    low_o = lax.broadcasted_iota(jnp.int32, (8, N), 0) < HG

    def matvec(b, g, t_a, t_r):
        lanes = pl.ds(g * GL, GL)
        a_b = jnp.broadcast_to(al_ref[b, pl.ds(t_a, 1), lanes], (8, GL))
        r_b = jnp.broadcast_to(r_ref[b, pl.ds(t_r, 1), lanes], (8, GL))
        x = (jnp.where(low_x, r_b, a_b) if g % 2 == 0 else jnp.where(low_x, a_b, r_b)) * head_mask
        s = s_scr[b, g]
        return lax.dot_general(x.astype(BF16), s.astype(BF16), (((1,), (1,)), ((), ())),
                               preferred_element_type=F32)

    def step(t, carry):
        tm1 = jnp.maximum(t - 1, 0)
        for b in range(WKV_SEQS):
            for j in range(NG // 2):
                vt = vh_ref[b, t, 8 * j:8 * j + 8, :]
                outs = []
                for g in (2 * j, 2 * j + 1):
                    lanes = pl.ds(g * GL, GL)
                    out = matvec(b, g, t, tm1)
                    outs.append(out)
                    lt = jnp.where(low_o, vt, out) if g % 2 == 0 else jnp.where(low_o, out, vt)
                    k_b = jnp.broadcast_to(k_ref[b, pl.ds(t, 1), lanes], (8, GL))
                    be_b = jnp.broadcast_to(be_ref[b, pl.ds(t, 1), lanes], (8, GL))
                    q = (jnp.where(low_x, k_b, be_b) if g % 2 == 0
                         else jnp.where(low_x, be_b, k_b)) * head_mask
                    ds = lax.dot_general(lt.astype(BF16), q.astype(BF16), (((0,), (0,)), ((), ())),
                                         preferred_element_type=F32)
                    w_b = w_ref[b, pl.ds(t, 1), lanes]
                    s_scr[b, g] = s_scr[b, g] * w_b + ds
                y_ref[b, tm1, 8 * j:8 * j + 8, :] = jnp.where(low_o, outs[0], outs[1])
        return carry

    lax.fori_loop(0, tb, step, 0)

    for b in range(WKV_SEQS):
        for j in range(NG // 2):
            o0 = matvec(b, 2 * j, tb - 1, tb - 1)
            o1 = matvec(b, 2 * j + 1, tb - 1, tb - 1)
            y_ref[b, tb - 1, 8 * j:8 * j + 8, :] = jnp.where(low_o, o0, o1)

    @pl.when(ti == pl.num_programs(1) - 1)
    def _():
        so_ref[...] = s_scr[...]


def _wkv_call(al, be, k2, r, w, vh, s0, bn, t, tb):
    seq = pl.BlockSpec((WKV_SEQS, tb, D), lambda i, j: (i, j, 0))
    hv = pl.BlockSpec((WKV_SEQS, tb, H, N), lambda i, j: (i, j, 0, 0))
    st = pl.BlockSpec((WKV_SEQS, NG, N, GL), lambda i, j: (i, 0, 0, 0))
    return pl.pallas_call(
        functools.partial(_wkv_body, tb=tb),
        grid=(bn // WKV_SEQS, t // tb),
        in_specs=[seq, seq, seq, seq, seq, hv, st],
        out_specs=[hv, st],
        out_shape=[jax.ShapeDtypeStruct((bn, t, H, N), F32),
                   jax.ShapeDtypeStruct((bn, NG, N, GL), F32)],
        scratch_shapes=[pltpu.VMEM((WKV_SEQS, NG, N, GL), F32)],
        compiler_params=_params(("arbitrary", "arbitrary")),
        name="wkv7_scan",
    )(al, be, k2, r, w, vh, s0)


def _post_body(y_ref, bo_ref, g_ref, sga_ref, yb_ref, x_ref, gtm_ref, scf_ref, shf_ref,
               gng_ref, gnb_ref, wo_ref, wout_ref, postg_ref, preg_ref,
               wrh_ref, wrl_ref, rb_ref, seg_ref, exp_ref,
               x1_o, hf_o, gates_o):
    seg = seg_ref[...]
    exp = exp_ref[...]
    y = y_ref[...]
    mu = _head_sum(y, seg, exp) * (1.0 / N)
    cen = y - mu
    var = _head_sum(cen * cen, seg, exp) * (1.0 / N)
    yn = cen * lax.rsqrt(var + GN_EPS) * gng_ref[...] + gnb_ref[...] + bo_ref[...]
    ya = _dot((yn * g_ref[...]).astype(BF16), wo_ref[...])
    merged = sga_ref[...] * ya + yb_ref[...]
    z = _dot(merged.astype(BF16), wout_ref[...])
    zn = z * lax.rsqrt(jnp.mean(z * z, axis=-1, keepdims=True) + RMS_EPS) * postg_ref[...]
    x1 = x_ref[...] + gtm_ref[...] * zn
    x1_o[...] = x1
    hn = x1 * lax.rsqrt(jnp.mean(x1 * x1, axis=-1, keepdims=True) + RMS_EPS) * preg_ref[...]
    hf = hn * (1.0 + scf_ref[...]) + shf_ref[...]
    hf_o[...] = hf.astype(BF16)

    hh, hl = _split_bf16(hf)
    logits = _dot(hh, wrh_ref[...]) + _dot(hl, wrh_ref[...]) + _dot(hh, wrl_ref[...])
    scores = _sigmoid(logits)
    lane = lax.broadcasted_iota(jnp.int32, scores.shape, 1).astype(F32)
    neg = jnp.float32(-jnp.inf)
    sb = jnp.where(lane < E, scores + rb_ref[...], neg)
    picked = jnp.zeros(scores.shape, F32)
    for _ in range(TOP_K):
        m = jnp.max(sb, axis=-1, keepdims=True)
        first = jnp.min(jnp.where(sb == m, lane, float(LANES)), axis=-1, keepdims=True)
        pick = lane == first
        picked = jnp.where(pick, 1.0, picked)
        sb = jnp.where(pick, neg, sb)
    sel = scores * picked
    gates_o[...] = sel / jnp.sum(sel, axis=-1, keepdims=True) * ROUTED_SCALE


def _post_call(y, bo, g, sga, yb, x, gtm, scf, shf, pw, per_token, tm, rows_per_seq):
    ntok = x.shape[0]
    tok = pl.BlockSpec((tm, D), lambda i: (i, 0))
    row = lambda shape: pl.BlockSpec(shape, lambda i: (0,) * len(shape))
    mspec = _mod_spec(per_token, tm, rows_per_seq)
    return pl.pallas_call(
        _post_body,
        grid=(ntok // tm,),
        in_specs=[tok] * 6 + [mspec] * 3 + [
            row((1, D)), row((1, D)), row((D, D)), row((D, D)), row((1, D)), row((1, D)),
            row((D, LANES)), row((D, LANES)), row((1, LANES)), row((D, LANES)), row((LANES, D))],
        out_specs=[tok, tok, pl.BlockSpec((tm, LANES), lambda i: (i, 0))],
        out_shape=[jax.ShapeDtypeStruct((ntok, D), F32),
                   jax.ShapeDtypeStruct((ntok, D), BF16),
                   jax.ShapeDtypeStruct((ntok, LANES), F32)],
        compiler_params=_params(("arbitrary",)),
        name="mix_post_router",
    )(y, bo, g, sga, yb, x, gtm, scf, shf, pw["gn_g"], pw["gn_b"], pw["w_o_rwkv"], pw["w_out"],
      pw["mix_post_g"], pw["ffn_pre_g"], pw["wr_hi"], pw["wr_lo"], pw["router_bias"],
      pw["seg"], pw["exp"])


MOE_EB = 4


def _moe_body(hf_ref, gates_ref, x1_ref, gtf_ref, postg_ref, wsg_ref, wsu_ref, wsd_ref,
              wg_ref, wu_ref, wd_ref, o_ref, acc_scr):
    e0 = pl.program_id(1)
    t = hf_ref[...]

    @pl.when(e0 == 0)
    def _():
        hg = _dot(t, wsg_ref[...])
        he = hg * _sigmoid(hg) * _dot(t, wsu_ref[...])
        acc_scr[...] = _dot(he.astype(BF16), wsd_ref[...])

    gates = gates_ref[...]
    lane = lax.broadcasted_iota(jnp.int32, gates.shape, 1)
    acc = acc_scr[...]
    for i in range(MOE_EB):
        gcol = jnp.sum(jnp.where(lane == e0 * MOE_EB + i, gates, 0.0), axis=-1, keepdims=True)
        hg = _dot(t, wg_ref[i])
        he = hg * _sigmoid(hg) * _dot(t, wu_ref[i])
        acc = acc + gcol * _dot(he.astype(BF16), wd_ref[i])
    acc_scr[...] = acc

    @pl.when(e0 == pl.num_programs(1) - 1)
    def _():
        out = acc_scr[...]
        on = out * lax.rsqrt(jnp.mean(out * out, axis=-1, keepdims=True) + RMS_EPS) * postg_ref[...]
        o_ref[...] = x1_ref[...] + gtf_ref[...] * on


def _moe_call(hf, gates, x1, gtf, pw, per_token, tm, rows_per_seq):
    ntok = hf.shape[0]
    tok = pl.BlockSpec((tm, D), lambda i, e: (i, 0))
    row = lambda shape: pl.BlockSpec(shape, lambda i, e: (0,) * len(shape))
    mspec = _mod_spec(per_token, tm, rows_per_seq)
    return pl.pallas_call(
        _moe_body,
        grid=(ntok // tm, E // MOE_EB),
        in_specs=[tok, pl.BlockSpec((tm, LANES), lambda i, e: (i, 0)), tok, mspec,
                  row((1, D)), row((D, DE)), row((D, DE)), row((DE, D)),
                  pl.BlockSpec((MOE_EB, D, DE), lambda i, e: (e, 0, 0)),
                  pl.BlockSpec((MOE_EB, D, DE), lambda i, e: (e, 0, 0)),
                  pl.BlockSpec((MOE_EB, DE, D), lambda i, e: (e, 0, 0))],
        out_specs=tok,
        out_shape=jax.ShapeDtypeStruct((ntok, D), F32),
        scratch_shapes=[pltpu.VMEM((tm, D), F32)],
        compiler_params=_params(("arbitrary", "arbitrary")),
        name="moe_ffn",
    )(hf, gates, x1, gtf, pw["ffn_post_g"], pw["w_sh_gate"], pw["w_sh_up"], pw["w_sh_down"],
      pw["w_exp_gate"], pw["w_exp_up"], pw["w_exp_down"])


def _pad_rwkv_cols(a):
    pad = lambda z, n: jnp.pad(z, [(0, 0)] * (z.ndim - 1) + [(0, n - z.shape[-1])])
    return jnp.concatenate([a[..., :3072], pad(a[..., 3072:3136], 128), pad(a[..., 3136:3200], 128),
                            pad(a[..., 3200:3360], 256)], axis=-1)


def _unpad_rwkv_cols(a):
    return jnp.concatenate([a[..., :3072], a[..., C_XW:C_XW + 64], a[..., C_XA:C_XA + 64],
                            a[..., C_XG:C_XG + 160]], axis=-1)


def _layer(x, mod, wkv0, shift0, conv0, pw, per_token, tiles):
    bn, t, _ = x.shape
    ntok = bn * t
    xf = x.reshape(ntok, D)
    chunks = [mod[:, i * D:(i + 1) * D] for i in range(6)]
    if per_token:
        chunks = [jnp.repeat(c, t, axis=0).reshape(1, ntok, D) for c in chunks]
    else:
        chunks = [c.reshape(bn, 1, D) for c in chunks]
    sh_m, sc_m, gt_m, sh_f, sc_f, gt_f = chunks

    proj = _proj_call(xf, sc_m, sh_m, pw["mix_pre_g"], pw["w_in"], per_token, tiles["proj"], t)
    proj = proj.reshape(bn, t, NIN)
    shift_out = _unpad_rwkv_cols(proj[:, t - 1, :NRW])

    shift0_p = _pad_rwkv_cols(shift0).reshape(bn, 1, NRW)
    al, be, k2, r, w, v, g, bo, sga, yb, conv_out = _prep_call(
        proj, shift0_p, conv0, pw, bn, t, tiles["prep"])

    flat = lambda a: a.reshape(ntok, D)
    vh = v.reshape(bn, t, H, N)
    s0 = wkv0.reshape(bn, NG, HG, N, N).transpose(0, 1, 3, 2, 4).reshape(bn, NG, N, GL)
    yh, s_out = _wkv_call(al, be, k2, r, w, vh, s0, bn, t, tiles["wkv"])
    wkv_out = s_out.reshape(bn, NG, N, HG, N).transpose(0, 1, 3, 2, 4).reshape(bn, H, N, N)
    y = yh.reshape(ntok, D)

    x1, hf, gates = _post_call(y, flat(bo), flat(g), flat(sga), flat(yb), xf, gt_m, sc_f, sh_f,
                               pw, per_token, tiles["post"], t)
    x2 = _moe_call(hf, gates, x1, gt_f, pw, per_token, tiles["moe"], t)
    return x2.reshape(bn, t, D), wkv_out, shift_out, conv_out


def kernel(x_prompt, x_sample, state_wkv, state_shift, state_conv, c_prompt, c_sample, w_ada, b_ada, mix_pre_g, mix_post_g, w_in, mu_shift, w0, w_decay2, a0, w_aaa2, w_gate2, k_k, k_a, r_k, gn_g, gn_b, w_o_rwkv, dw_w, dw_b, conv_ln_g, conv_ln_b, w_conv_out, w_out, ffn_pre_g, ffn_post_g, w_router, router_bias, w_exp_gate, w_exp_up, w_exp_down, w_sh_gate, w_sh_up, w_sh_down):
    bp = x_prompt.shape[0]
    bs = x_sample.shape[0]
    row = lambda a: a.reshape(1, -1).astype(F32)
    padr = lambda a, n: jnp.pad(a, ((0, n - a.shape[0]), (0, 0)))
    padc = lambda a, n: jnp.pad(a, ((0, 0), (0, n - a.shape[1])))

    w_in_p = jnp.concatenate(
        [w_in[:, :3072], padc(w_in[:, 3072:3136], 128), padc(w_in[:, 3136:3200], 128),
         padc(w_in[:, 3200:3360], 256), w_in[:, 3360:]], axis=1).astype(BF16)
    head_of_col = jnp.arange(D, dtype=jnp.int32) // N
    seg = (head_of_col[:, None] == jnp.arange(LANES, dtype=jnp.int32)[None, :]).astype(BF16)
    wr = padc(w_router, LANES)
    wr_hi = wr.astype(BF16)
    pw = {
        "w_in": w_in_p,
        "mix_pre_g": row(mix_pre_g), "mix_post_g": row(mix_post_g),
        "ffn_pre_g": row(ffn_pre_g), "ffn_post_g": row(ffn_post_g),
        "mu": row(_pad_rwkv_cols(mu_shift)), "w0": row(w0), "a0": row(a0),
        "k_k": row(k_k), "k_a": row(k_a), "r_k": row(r_k),
        "wd2": padr(w_decay2, 128).astype(BF16), "wa2": padr(w_aaa2, 128).astype(BF16),
        "wg2": padr(w_gate2, 256).astype(BF16),
        "dw_w": dw_w, "dw_b": row(dw_b), "ln_g": row(conv_ln_g), "ln_b": row(conv_ln_b),
        "w_conv_out": w_conv_out.astype(BF16),
        "gn_g": row(gn_g), "gn_b": row(gn_b),
        "w_o_rwkv": w_o_rwkv.astype(BF16), "w_out": w_out.astype(BF16),
        "wr_hi": wr_hi, "wr_lo": (wr - wr_hi.astype(F32)).astype(BF16),
        "router_bias": padc(row(router_bias), LANES),
        "seg": seg, "exp": seg.T,
        "w_sh_gate": w_sh_gate.astype(BF16), "w_sh_up": w_sh_up.astype(BF16),
        "w_sh_down": w_sh_down.astype(BF16),
        "w_exp_gate": w_exp_gate.astype(BF16), "w_exp_up": w_exp_up.astype(BF16),
        "w_exp_down": w_exp_down.astype(BF16),
    }

    mod = _mod_call(jnp.concatenate([c_prompt, c_sample], axis=0), w_ada, b_ada)
    zeros = lambda shape: jnp.zeros(shape, x_prompt.dtype)
    yp, wkv_p, shift_p, conv_p = _layer(
        x_prompt, mod[:bp], zeros((bp, H, N, N)), zeros((bp, state_shift.shape[1])),
        zeros((bp, CONV_W - 1, D)), pw, per_token=False,
        tiles={"proj": 1024, "prep": 256, "wkv": 32, "post": 256, "moe": 512})
    ts = x_sample.shape[1]
    ys, wkv_s, shift_s, conv_s = _layer(
        x_sample, mod[bp:], state_wkv, state_shift, state_conv, pw, per_token=True,
        tiles={"proj": bs * ts, "prep": ts, "wkv": ts, "post": 256, "moe": bs * ts})
    return (yp, ys, wkv_p, shift_p, conv_p, wkv_s, shift_s, conv_s)
```

```python
import functools

import jax
import jax.numpy as jnp
from jax import lax
from jax.experimental import pallas as pl
from jax.experimental.pallas import tpu as pltpu

F32 = jnp.float32
BF16 = jnp.bfloat16

D = 1024
H = 16
N = 64
E = 64
DE = 256
TOP_K = 6
ROUTED_SCALE = 2.5
CONV_W = 31
RMS_EPS = 1e-6
LN_EPS = 1e-5
GN_EPS = N * 1e-5

C_R, C_K, C_V = 0, 1024, 2048
C_XW, C_XA, C_XG = 3072, 3200, 3328
NRW = 3584
C_CVA, C_CVB, C_GA, C_GB = 3584, 4608, 5632, 6656
NIN = 7680

LANES = 128
HG = 4
GL = HG * N
NG = H // HG
WKV_SEQS = 8
VMEM_LIMIT = 56 * 1024 * 1024


def _dot(a, b):
    return jnp.dot(a, b, preferred_element_type=F32)


def _sigmoid(x):
    return jax.nn.sigmoid(x)


def _split_bf16(x):
    hi = x.astype(BF16)
    lo = (x - hi.astype(F32)).astype(BF16)
    return hi, lo


def _head_sum(x, seg, exp):
    hi, lo = _split_bf16(x)
    s = _dot(hi, seg) + _dot(lo, seg)
    shi, slo = _split_bf16(s)
    return _dot(shi, exp) + _dot(slo, exp)


def _params(sem, vmem=VMEM_LIMIT):
    return pltpu.CompilerParams(dimension_semantics=sem, vmem_limit_bytes=vmem)


def _mod_body(c_ref, w_ref, b_ref, o_ref):
    c = c_ref[...]
    s = (c * _sigmoid(c)).astype(BF16)
    o_ref[...] = _dot(s, w_ref[...].astype(BF16)) + b_ref[...]


def _mod_call(c_all, w_ada, b_ada):
    rows = c_all.shape[0]
    tn = 512
    return pl.pallas_call(
        _mod_body,
        grid=(6 * D // tn,),
        in_specs=[pl.BlockSpec((rows, D), lambda j: (0, 0)),
                  pl.BlockSpec((D, tn), lambda j: (0, j)),
                  pl.BlockSpec((1, tn), lambda j: (0, j))],
        out_specs=pl.BlockSpec((rows, tn), lambda j: (0, j)),
        out_shape=jax.ShapeDtypeStruct((rows, 6 * D), F32),
        compiler_params=_params(("arbitrary",)),
        name="ada_mod",
    )(c_all, w_ada, b_ada.reshape(1, 6 * D))


def _mod_spec(per_token, tm, rows_per_seq):
    if per_token:
        return pl.BlockSpec((None, tm, D), lambda i, *_: (0, i, 0))
    return pl.BlockSpec((None, 1, D), lambda i, *_: (i * tm // rows_per_seq, 0, 0))


def _proj_body(x_ref, sc_ref, sh_ref, g_ref, w_ref, o_ref, h_scr):
    @pl.when(pl.program_id(1) == 0)
    def _():
        x = x_ref[...]
        y = x * lax.rsqrt(jnp.mean(x * x, axis=-1, keepdims=True) + RMS_EPS) * g_ref[...]
        h_scr[...] = (y * (1.0 + sc_ref[...]) + sh_ref[...]).astype(BF16)

    o_ref[...] = _dot(h_scr[...], w_ref[...])


def _proj_call(x, sc, sh, g, w_in_p, per_token, tm, rows_per_seq):
    ntok = x.shape[0]
    tn = 1280
    mspec = _mod_spec(per_token, tm, rows_per_seq)
    return pl.pallas_call(
        _proj_body,
        grid=(ntok // tm, NIN // tn),
        in_specs=[pl.BlockSpec((tm, D), lambda i, j: (i, 0)),
                  mspec, mspec,
                  pl.BlockSpec((1, D), lambda i, j: (0, 0)),
                  pl.BlockSpec((D, tn), lambda i, j: (0, j))],
        out_specs=pl.BlockSpec((tm, tn), lambda i, j: (i, j)),
        out_shape=jax.ShapeDtypeStruct((ntok, NIN), F32),
        scratch_shapes=[pltpu.VMEM((tm, D), BF16)],
        compiler_params=_params(("arbitrary", "arbitrary")),
        name="in_proj",
    )(x, sc, sh, g, w_in_p)


def _prep_body(proj_ref, sh0_ref, cv0_ref, mu_ref, w0_ref, a0_ref, kk_ref, ka_ref, rk_ref,
               wd2_ref, wa2_ref, wg2_ref, dww_ref, dwb_ref, lng_ref, lnb_ref, wco_ref,
               seg_ref, exp_ref,
               al_o, be_o, k_o, r_o, w_o, v_o, g_o, bo_o, sga_o, yb_o, cv_o,
               buf_scr, ext_scr, *, tm):
    ti = pl.program_id(1)

    @pl.when(ti == 0)
    def _():
        buf_scr[7:8, :] = sh0_ref[...]
        ext_scr[0:2, :] = jnp.zeros((2, D), F32)
        ext_scr[2:32, :] = cv0_ref[...]

    p = proj_ref[:, 0:NRW]
    buf_scr[8:8 + tm, :] = p
    prev = buf_scr[7:7 + tm, :]
    xs = p + (prev - p) * mu_ref[...]
    buf_scr[7:8, :] = p[tm - 1:tm, :]

    r = xs[:, C_R:C_R + D]
    k = xs[:, C_K:C_K + D]
    v = xs[:, C_V:C_V + D]
    xw = xs[:, C_XW:C_XW + 128]
    xa = xs[:, C_XA:C_XA + 128]
    xg = xs[:, C_XG:C_XG + 256]

    lw = _dot(jnp.tanh(xw).astype(BF16), wd2_ref[...])
    z = -(w0_ref[...] + lw)
    softplus = jnp.maximum(z, 0.0) + jnp.log(1.0 + jnp.exp(-jnp.abs(z)))
    decay = jnp.exp(-jnp.exp(-softplus - 0.5))
    a = _sigmoid(a0_ref[...] + _dot(xa.astype(BF16), wa2_ref[...]))
    g_o[...] = _dot(_sigmoid(xg).astype(BF16), wg2_ref[...])

    seg = seg_ref[...]
    exp = exp_ref[...]
    kk = k * kk_ref[...]
    kk = kk / jnp.maximum(jnp.sqrt(_head_sum(kk * kk, seg, exp)), 1e-12)
    k2 = k * (1.0 + (a - 1.0) * ka_ref[...])
    bo_o[...] = _head_sum(r * k2 * rk_ref[...], seg, exp) * v
    al_o[...] = -kk
    be_o[...] = kk * a
    k_o[...] = k2
    r_o[...] = r
    w_o[...] = decay
    v_o[...] = v

    glu = proj_ref[:, C_CVA:C_CVA + D] * _sigmoid(proj_ref[:, C_CVB:C_CVB + D])
    ext_scr[32:32 + tm, :] = glu
    acc = jnp.zeros((tm, D), F32) + dwb_ref[...]
    for j in range(CONV_W):
        acc = acc + ext_scr[2 + j:2 + j + tm, :] * dww_ref[j:j + 1, :]
    tail = ext_scr[tm:tm + 32, :]
    cv_o[...] = tail[2:32, :]
    ext_scr[0:32, :] = tail
    mean = jnp.mean(acc, axis=-1, keepdims=True)
    cen = acc - mean
    var = jnp.mean(cen * cen, axis=-1, keepdims=True)
    ln = cen * lax.rsqrt(var + LN_EPS) * lng_ref[...] + lnb_ref[...]
    u = (ln * _sigmoid(ln)).astype(BF16)
    yb = _dot(u, wco_ref[...])
    sga_o[...] = _sigmoid(proj_ref[:, C_GA:C_GA + D])
    yb_o[...] = _sigmoid(proj_ref[:, C_GB:C_GB + D]) * yb


def _prep_call(proj, shift0_p, conv0, pw, bn, t, tm):
    nt = t // tm
    row = lambda shape: pl.BlockSpec(shape, lambda b, i: (0,) * len(shape))
    tok = pl.BlockSpec((None, tm, D), lambda b, i: (b, i, 0))
    ins = [pl.BlockSpec((None, tm, NIN), lambda b, i: (b, i, 0)),
           pl.BlockSpec((None, 1, NRW), lambda b, i: (b, 0, 0)),
           pl.BlockSpec((None, CONV_W - 1, D), lambda b, i: (b, 0, 0)),
           row((1, NRW)), row((1, D)), row((1, D)), row((1, D)), row((1, D)), row((1, D)),
           row((128, D)), row((128, D)), row((256, D)),
           row((CONV_W, D)), row((1, D)), row((1, D)), row((1, D)), row((D, D)),
           row((D, LANES)), row((LANES, D))]
    outs = [tok] * 10 + [pl.BlockSpec((None, CONV_W - 1, D), lambda b, i: (b, 0, 0))]
    out_shape = [jax.ShapeDtypeStruct((bn, t, D), F32)] * 10 + [
        jax.ShapeDtypeStruct((bn, CONV_W - 1, D), F32)]
    return pl.pallas_call(
        functools.partial(_prep_body, tm=tm),
        grid=(bn, nt),
        in_specs=ins,
        out_specs=outs,
        out_shape=out_shape,
        scratch_shapes=[pltpu.VMEM((tm + 8, NRW), F32), pltpu.VMEM((tm + 32, D), F32)],
        compiler_params=_params(("arbitrary", "arbitrary")),
        name="rwkv_prep_conv",
    )(proj, shift0_p, conv0, pw["mu"], pw["w0"], pw["a0"], pw["k_k"], pw["k_a"], pw["r_k"],
      pw["wd2"], pw["wa2"], pw["wg2"], pw["dw_w"], pw["dw_b"], pw["ln_g"], pw["ln_b"],
      pw["w_conv_out"], pw["seg"], pw["exp"])


def _wkv_body(al_ref, be_ref, k_ref, r_ref, w_ref, vh_ref, s0_ref, y_ref, so_ref, s_scr, *, tb):
    ti = pl.program_id(1)

    @pl.when(ti == 0)
    def _():
        s_scr[...] = s0_ref[...]

    row_x = lax.broadcasted_iota(jnp.int32, (8, GL), 0)
    lane_x = lax.broadcasted_iota(jnp.int32, (8, GL), 1)
    head_mask = (lane_x // N == row_x % HG).astype(F32)
    low_x = row_x < HG
    low_o = lax.broadcasted_iota(jnp.int32, (8, N), 0) < HG

    def matvec(b, g, t_a, t_r):
        lanes = pl.ds(g * GL, GL)
        a_b = jnp.broadcast_to(al_ref[b, pl.ds(t_a, 1), lanes], (8, GL))
        r_b = jnp.broadcast_to(r_ref[b, pl.ds(t_r, 1), lanes], (8, GL))
        x = (jnp.where(low_x, r_b, a_b) if g % 2 == 0 else jnp.where(low_x, a_b, r_b)) * head_mask
        s = s_scr[b, g]
        return lax.dot_general(x.astype(BF16), s.astype(BF16), (((1,), (1,)), ((), ())),
                               preferred_element_type=F32)

    def step(t, carry):
        tm1 = jnp.maximum(t - 1, 0)
        mv = {(b, g): matvec(b, g, t, tm1) for b in range(WKV_SEQS) for g in range(NG)}
        for b in range(WKV_SEQS):
            for j in range(NG // 2):
                vt = vh_ref[b, t, 8 * j:8 * j + 8, :]
                outs = []
                for g in (2 * j, 2 * j + 1):
                    lanes = pl.ds(g * GL, GL)
                    out = mv[b, g]
                    outs.append(out)
                    lt = jnp.where(low_o, vt, out) if g % 2 == 0 else jnp.where(low_o, out, vt)
                    k_b = jnp.broadcast_to(k_ref[b, pl.ds(t, 1), lanes], (8, GL))
                    be_b = jnp.broadcast_to(be_ref[b, pl.ds(t, 1), lanes], (8, GL))
                    q = (jnp.where(low_x, k_b, be_b) if g % 2 == 0
                         else jnp.where(low_x, be_b, k_b)) * head_mask
                    ds = lax.dot_general(lt.astype(BF16), q.astype(BF16), (((0,), (0,)), ((), ())),
                                         preferred_element_type=F32)
                    w_b = w_ref[b, pl.ds(t, 1), lanes]
                    s_scr[b, g] = s_scr[b, g] * w_b + ds
                y_ref[b, tm1, 8 * j:8 * j + 8, :] = jnp.where(low_o, outs[0], outs[1])
        return carry

    lax.fori_loop(0, tb, step, 0)

    for b in range(WKV_SEQS):
        for j in range(NG // 2):
            o0 = matvec(b, 2 * j, tb - 1, tb - 1)
            o1 = matvec(b, 2 * j + 1, tb - 1, tb - 1)
            y_ref[b, tb - 1, 8 * j:8 * j + 8, :] = jnp.where(low_o, o0, o1)

    @pl.when(ti == pl.num_programs(1) - 1)
    def _():
        so_ref[...] = s_scr[...]


def _wkv_call(al, be, k2, r, w, vh, s0, bn, t, tb):
    seq = pl.BlockSpec((WKV_SEQS, tb, D), lambda i, j: (i, j, 0))
    hv = pl.BlockSpec((WKV_SEQS, tb, H, N), lambda i, j: (i, j, 0, 0))
    st = pl.BlockSpec((WKV_SEQS, NG, N, GL), lambda i, j: (i, 0, 0, 0))
    return pl.pallas_call(
        functools.partial(_wkv_body, tb=tb),
        grid=(bn // WKV_SEQS, t // tb),
        in_specs=[seq, seq, seq, seq, seq, hv, st],
        out_specs=[hv, st],
        out_shape=[jax.ShapeDtypeStruct((bn, t, H, N), F32),
                   jax.ShapeDtypeStruct((bn, NG, N, GL), F32)],
        scratch_shapes=[pltpu.VMEM((WKV_SEQS, NG, N, GL), F32)],
        compiler_params=_params(("arbitrary", "arbitrary")),
        name="wkv7_scan",
    )(al, be, k2, r, w, vh, s0)


def _post_body(y_ref, bo_ref, g_ref, sga_ref, yb_ref, x_ref, gtm_ref, scf_ref, shf_ref,
               gng_ref, gnb_ref, wo_ref, wout_ref, postg_ref, preg_ref,
               wrh_ref, wrl_ref, rb_ref, seg_ref, exp_ref,
               x1_o, hf_o, gates_o):
    seg = seg_ref[...]
    exp = exp_ref[...]
    y = y_ref[...]
    mu = _head_sum(y, seg, exp) * (1.0 / N)
    cen = y - mu
    var = _head_sum(cen * cen, seg, exp) * (1.0 / N)
    yn = cen * lax.rsqrt(var + GN_EPS) * gng_ref[...] + gnb_ref[...] + bo_ref[...]
    ya = _dot((yn * g_ref[...]).astype(BF16), wo_ref[...])
    merged = sga_ref[...] * ya + yb_ref[...]
    z = _dot(merged.astype(BF16), wout_ref[...])
    zn = z * lax.rsqrt(jnp.mean(z * z, axis=-1, keepdims=True) + RMS_EPS) * postg_ref[...]
    x1 = x_ref[...] + gtm_ref[...] * zn
    x1_o[...] = x1
    hn = x1 * lax.rsqrt(jnp.mean(x1 * x1, axis=-1, keepdims=True) + RMS_EPS) * preg_ref[...]
    hf = hn * (1.0 + scf_ref[...]) + shf_ref[...]
    hf_o[...] = hf.astype(BF16)

    hh, hl = _split_bf16(hf)
    logits = _dot(hh, wrh_ref[...]) + _dot(hl, wrh_ref[...]) + _dot(hh, wrl_ref[...])
    scores = _sigmoid(logits)
    lane = lax.broadcasted_iota(jnp.int32, scores.shape, 1).astype(F32)
    neg = jnp.float32(-jnp.inf)
    sb = jnp.where(lane < E, scores + rb_ref[...], neg)
    picked = jnp.zeros(scores.shape, F32)
    for _ in range(TOP_K):
        m = jnp.max(sb, axis=-1, keepdims=True)
        first = jnp.min(jnp.where(sb == m, lane, float(LANES)), axis=-1, keepdims=True)
        pick = lane == first
        picked = jnp.where(pick, 1.0, picked)
        sb = jnp.where(pick, neg, sb)
    sel = scores * picked
    gates_o[...] = sel / jnp.sum(sel, axis=-1, keepdims=True) * ROUTED_SCALE


def _post_call(y, bo, g, sga, yb, x, gtm, scf, shf, pw, per_token, tm, rows_per_seq):
    ntok = x.shape[0]
    tok = pl.BlockSpec((tm, D), lambda i: (i, 0))
    row = lambda shape: pl.BlockSpec(shape, lambda i: (0,) * len(shape))
    mspec = _mod_spec(per_token, tm, rows_per_seq)
    return pl.pallas_call(
        _post_body,
        grid=(ntok // tm,),
        in_specs=[tok] * 6 + [mspec] * 3 + [
            row((1, D)), row((1, D)), row((D, D)), row((D, D)), row((1, D)), row((1, D)),
            row((D, LANES)), row((D, LANES)), row((1, LANES)), row((D, LANES)), row((LANES, D))],
        out_specs=[tok, tok, pl.BlockSpec((tm, LANES), lambda i: (i, 0))],
        out_shape=[jax.ShapeDtypeStruct((ntok, D), F32),
                   jax.ShapeDtypeStruct((ntok, D), BF16),
                   jax.ShapeDtypeStruct((ntok, LANES), F32)],
        compiler_params=_params(("arbitrary",)),
        name="mix_post_router",
    )(y, bo, g, sga, yb, x, gtm, scf, shf, pw["gn_g"], pw["gn_b"], pw["w_o_rwkv"], pw["w_out"],
      pw["mix_post_g"], pw["ffn_pre_g"], pw["wr_hi"], pw["wr_lo"], pw["router_bias"],
      pw["seg"], pw["exp"])


MOE_EB = 4


def _moe_body(hf_ref, gates_ref, x1_ref, gtf_ref, postg_ref, wsg_ref, wsu_ref, wsd_ref,
              wg_ref, wu_ref, wd_ref, o_ref, acc_scr):
    e0 = pl.program_id(1)
    t = hf_ref[...]

    @pl.when(e0 == 0)
    def _():
        hg = _dot(t, wsg_ref[...])
        he = hg * _sigmoid(hg) * _dot(t, wsu_ref[...])
        acc_scr[...] = _dot(he.astype(BF16), wsd_ref[...])

    gates = gates_ref[...]
    lane = lax.broadcasted_iota(jnp.int32, gates.shape, 1)
    acc = acc_scr[...]
    for i in range(MOE_EB):
        gcol = jnp.sum(jnp.where(lane == e0 * MOE_EB + i, gates, 0.0), axis=-1, keepdims=True)
        hg = _dot(t, wg_ref[i])
        he = hg * _sigmoid(hg) * _dot(t, wu_ref[i])
        acc = acc + gcol * _dot(he.astype(BF16), wd_ref[i])
    acc_scr[...] = acc

    @pl.when(e0 == pl.num_programs(1) - 1)
    def _():
        out = acc_scr[...]
        on = out * lax.rsqrt(jnp.mean(out * out, axis=-1, keepdims=True) + RMS_EPS) * postg_ref[...]
        o_ref[...] = x1_ref[...] + gtf_ref[...] * on


def _moe_call(hf, gates, x1, gtf, pw, per_token, tm, rows_per_seq):
    ntok = hf.shape[0]
    tok = pl.BlockSpec((tm, D), lambda i, e: (i, 0))
    row = lambda shape: pl.BlockSpec(shape, lambda i, e: (0,) * len(shape))
    mspec = _mod_spec(per_token, tm, rows_per_seq)
    return pl.pallas_call(
        _moe_body,
        grid=(ntok // tm, E // MOE_EB),
        in_specs=[tok, pl.BlockSpec((tm, LANES), lambda i, e: (i, 0)), tok, mspec,
                  row((1, D)), row((D, DE)), row((D, DE)), row((DE, D)),
                  pl.BlockSpec((MOE_EB, D, DE), lambda i, e: (e, 0, 0)),
                  pl.BlockSpec((MOE_EB, D, DE), lambda i, e: (e, 0, 0)),
                  pl.BlockSpec((MOE_EB, DE, D), lambda i, e: (e, 0, 0))],
        out_specs=tok,
        out_shape=jax.ShapeDtypeStruct((ntok, D), F32),
        scratch_shapes=[pltpu.VMEM((tm, D), F32)],
        compiler_params=_params(("arbitrary", "arbitrary")),
        name="moe_ffn",
    )(hf, gates, x1, gtf, pw["ffn_post_g"], pw["w_sh_gate"], pw["w_sh_up"], pw["w_sh_down"],
      pw["w_exp_gate"], pw["w_exp_up"], pw["w_exp_down"])


def _pad_rwkv_cols(a):
    pad = lambda z, n: jnp.pad(z, [(0, 0)] * (z.ndim - 1) + [(0, n - z.shape[-1])])
    return jnp.concatenate([a[..., :3072], pad(a[..., 3072:3136], 128), pad(a[..., 3136:3200], 128),
                            pad(a[..., 3200:3360], 256)], axis=-1)


def _unpad_rwkv_cols(a):
    return jnp.concatenate([a[..., :3072], a[..., C_XW:C_XW + 64], a[..., C_XA:C_XA + 64],
                            a[..., C_XG:C_XG + 160]], axis=-1)


def _layer(x, mod, wkv0, shift0, conv0, pw, per_token, tiles):
    bn, t, _ = x.shape
    ntok = bn * t
    xf = x.reshape(ntok, D)
    chunks = [mod[:, i * D:(i + 1) * D] for i in range(6)]
    if per_token:
        chunks = [jnp.repeat(c, t, axis=0).reshape(1, ntok, D) for c in chunks]
    else:
        chunks = [c.reshape(bn, 1, D) for c in chunks]
    sh_m, sc_m, gt_m, sh_f, sc_f, gt_f = chunks

    proj = _proj_call(xf, sc_m, sh_m, pw["mix_pre_g"], pw["w_in"], per_token, tiles["proj"], t)
    proj = proj.reshape(bn, t, NIN)
    shift_out = _unpad_rwkv_cols(proj[:, t - 1, :NRW])

    shift0_p = _pad_rwkv_cols(shift0).reshape(bn, 1, NRW)
    al, be, k2, r, w, v, g, bo, sga, yb, conv_out = _prep_call(
        proj, shift0_p, conv0, pw, bn, t, tiles["prep"])

    flat = lambda a: a.reshape(ntok, D)
    vh = v.reshape(bn, t, H, N)
    s0 = wkv0.reshape(bn, NG, HG, N, N).transpose(0, 1, 3, 2, 4).reshape(bn, NG, N, GL)
    yh, s_out = _wkv_call(al, be, k2, r, w, vh, s0, bn, t, tiles["wkv"])
    wkv_out = s_out.reshape(bn, NG, N, HG, N).transpose(0, 1, 3, 2, 4).reshape(bn, H, N, N)
    y = yh.reshape(ntok, D)

    x1, hf, gates = _post_call(y, flat(bo), flat(g), flat(sga), flat(yb), xf, gt_m, sc_f, sh_f,
                               pw, per_token, tiles["post"], t)
    x2 = _moe_call(hf, gates, x1, gt_f, pw, per_token, tiles["moe"], t)
    return x2.reshape(bn, t, D), wkv_out, shift_out, conv_out


def kernel(x_prompt, x_sample, state_wkv, state_shift, state_conv, c_prompt, c_sample, w_ada, b_ada, mix_pre_g, mix_post_g, w_in, mu_shift, w0, w_decay2, a0, w_aaa2, w_gate2, k_k, k_a, r_k, gn_g, gn_b, w_o_rwkv, dw_w, dw_b, conv_ln_g, conv_ln_b, w_conv_out, w_out, ffn_pre_g, ffn_post_g, w_router, router_bias, w_exp_gate, w_exp_up, w_exp_down, w_sh_gate, w_sh_up, w_sh_down):
    bp = x_prompt.shape[0]
    bs = x_sample.shape[0]
    row = lambda a: a.reshape(1, -1).astype(F32)
    padr = lambda a, n: jnp.pad(a, ((0, n - a.shape[0]), (0, 0)))
    padc = lambda a, n: jnp.pad(a, ((0, 0), (0, n - a.shape[1])))

    w_in_p = jnp.concatenate(
        [w_in[:, :3072], padc(w_in[:, 3072:3136], 128), padc(w_in[:, 3136:3200], 128),
         padc(w_in[:, 3200:3360], 256), w_in[:, 3360:]], axis=1).astype(BF16)
    head_of_col = jnp.arange(D, dtype=jnp.int32) // N
    seg = (head_of_col[:, None] == jnp.arange(LANES, dtype=jnp.int32)[None, :]).astype(BF16)
    wr = padc(w_router, LANES)
    wr_hi = wr.astype(BF16)
    pw = {
        "w_in": w_in_p,
        "mix_pre_g": row(mix_pre_g), "mix_post_g": row(mix_post_g),
        "ffn_pre_g": row(ffn_pre_g), "ffn_post_g": row(ffn_post_g),
        "mu": row(_pad_rwkv_cols(mu_shift)), "w0": row(w0), "a0": row(a0),
        "k_k": row(k_k), "k_a": row(k_a), "r_k": row(r_k),
        "wd2": padr(w_decay2, 128).astype(BF16), "wa2": padr(w_aaa2, 128).astype(BF16),
        "wg2": padr(w_gate2, 256).astype(BF16),
        "dw_w": dw_w, "dw_b": row(dw_b), "ln_g": row(conv_ln_g), "ln_b": row(conv_ln_b),
        "w_conv_out": w_conv_out.astype(BF16),
        "gn_g": row(gn_g), "gn_b": row(gn_b),
        "w_o_rwkv": w_o_rwkv.astype(BF16), "w_out": w_out.astype(BF16),
        "wr_hi": wr_hi, "wr_lo": (wr - wr_hi.astype(F32)).astype(BF16),
        "router_bias": padc(row(router_bias), LANES),
        "seg": seg, "exp": seg.T,
        "w_sh_gate": w_sh_gate.astype(BF16), "w_sh_up": w_sh_up.astype(BF16),
        "w_sh_down": w_sh_down.astype(BF16),
        "w_exp_gate": w_exp_gate.astype(BF16), "w_exp_up": w_exp_up.astype(BF16),
        "w_exp_down": w_exp_down.astype(BF16),
    }

    mod = _mod_call(jnp.concatenate([c_prompt, c_sample], axis=0), w_ada, b_ada)
    zeros = lambda shape: jnp.zeros(shape, x_prompt.dtype)
    yp, wkv_p, shift_p, conv_p = _layer(
        x_prompt, mod[:bp], zeros((bp, H, N, N)), zeros((bp, state_shift.shape[1])),
        zeros((bp, CONV_W - 1, D)), pw, per_token=False,
        tiles={"proj": 1024, "prep": 256, "wkv": 32, "post": 256, "moe": 512})
    ts = x_sample.shape[1]
    ys, wkv_s, shift_s, conv_s = _layer(
        x_sample, mod[bp:], state_wkv, state_shift, state_conv, pw, per_token=True,
        tiles={"proj": bs * ts, "prep": ts, "wkv": ts, "post": 256, "moe": bs * ts})
    return (yp, ys, wkv_p, shift_p, conv_p, wkv_s, shift_s, conv_s)
```

```python
import functools

import jax
import jax.numpy as jnp
from jax import lax
from jax.experimental import pallas as pl
from jax.experimental.pallas import tpu as pltpu

F32 = jnp.float32
BF16 = jnp.bfloat16

D = 1024
H = 16
N = 64
E = 64
DE = 256
TOP_K = 6
ROUTED_SCALE = 2.5
CONV_W = 31
RMS_EPS = 1e-6
LN_EPS = 1e-5
GN_EPS = N * 1e-5

C_R, C_K, C_V = 0, 1024, 2048
C_XW, C_XA, C_XG = 3072, 3200, 3328
NRW = 3584
C_CVA, C_CVB, C_GA, C_GB = 3584, 4608, 5632, 6656
NIN = 7680

LANES = 128
HG = 4
GL = HG * N
NG = H // HG
WKV_SEQS = 8
WKV_UNROLL = 2
VMEM_LIMIT = 56 * 1024 * 1024


def _dot(a, b):
    return jnp.dot(a, b, preferred_element_type=F32)


def _sigmoid(x):
    return jax.nn.sigmoid(x)


def _split_bf16(x):
    hi = x.astype(BF16)
    lo = (x - hi.astype(F32)).astype(BF16)
    return hi, lo


def _head_sum(x, seg, exp):
    hi, lo = _split_bf16(x)
    s = _dot(hi, seg) + _dot(lo, seg)
    shi, slo = _split_bf16(s)
    return _dot(shi, exp) + _dot(slo, exp)


def _params(sem, vmem=VMEM_LIMIT):
    return pltpu.CompilerParams(dimension_semantics=sem, vmem_limit_bytes=vmem)


def _mod_body(c_ref, w_ref, b_ref, o_ref):
    c = c_ref[...]
    s = (c * _sigmoid(c)).astype(BF16)
    o_ref[...] = _dot(s, w_ref[...].astype(BF16)) + b_ref[...]


def _mod_call(c_all, w_ada, b_ada):
    rows = c_all.shape[0]
    tn = 512
    return pl.pallas_call(
        _mod_body,
        grid=(6 * D // tn,),
        in_specs=[pl.BlockSpec((rows, D), lambda j: (0, 0)),
                  pl.BlockSpec((D, tn), lambda j: (0, j)),
                  pl.BlockSpec((1, tn), lambda j: (0, j))],
        out_specs=pl.BlockSpec((rows, tn), lambda j: (0, j)),
        out_shape=jax.ShapeDtypeStruct((rows, 6 * D), F32),
        compiler_params=_params(("arbitrary",)),
        name="ada_mod",
    )(c_all, w_ada, b_ada.reshape(1, 6 * D))


def _mod_spec(per_token, tm, rows_per_seq):
    if per_token:
        return pl.BlockSpec((None, tm, D), lambda i, *_: (0, i, 0))
    return pl.BlockSpec((None, 1, D), lambda i, *_: (i * tm // rows_per_seq, 0, 0))


def _proj_body(x_ref, sc_ref, sh_ref, g_ref, w_ref, o_ref, h_scr):
    @pl.when(pl.program_id(1) == 0)
    def _():
        x = x_ref[...]
        y = x * lax.rsqrt(jnp.mean(x * x, axis=-1, keepdims=True) + RMS_EPS) * g_ref[...]
        h_scr[...] = (y * (1.0 + sc_ref[...]) + sh_ref[...]).astype(BF16)

    o_ref[...] = _dot(h_scr[...], w_ref[...])


def _proj_call(x, sc, sh, g, w_in_p, per_token, tm, rows_per_seq):
    ntok = x.shape[0]
    tn = 1280
    mspec = _mod_spec(per_token, tm, rows_per_seq)
    return pl.pallas_call(
        _proj_body,
        grid=(ntok // tm, NIN // tn),
        in_specs=[pl.BlockSpec((tm, D), lambda i, j: (i, 0)),
                  mspec, mspec,
                  pl.BlockSpec((1, D), lambda i, j: (0, 0)),
                  pl.BlockSpec((D, tn), lambda i, j: (0, j))],
        out_specs=pl.BlockSpec((tm, tn), lambda i, j: (i, j)),
        out_shape=jax.ShapeDtypeStruct((ntok, NIN), F32),
        scratch_shapes=[pltpu.VMEM((tm, D), BF16)],
        compiler_params=_params(("arbitrary", "arbitrary")),
        name="in_proj",
    )(x, sc, sh, g, w_in_p)


def _prep_body(proj_ref, sh0_ref, cv0_ref, mu_ref, w0_ref, a0_ref, kk_ref, ka_ref, rk_ref,
               wd2_ref, wa2_ref, wg2_ref, dww_ref, dwb_ref, lng_ref, lnb_ref, wco_ref,
               seg_ref, exp_ref,
               al_o, be_o, k_o, r_o, w_o, v_o, g_o, bo_o, sga_o, yb_o, cv_o,
               buf_scr, ext_scr, *, tm):
    ti = pl.program_id(1)

    @pl.when(ti == 0)
    def _():
        buf_scr[7:8, :] = sh0_ref[...]
        ext_scr[0:2, :] = jnp.zeros((2, D), F32)
        ext_scr[2:32, :] = cv0_ref[...]

    p = proj_ref[:, 0:NRW]
    buf_scr[8:8 + tm, :] = p
    prev = buf_scr[7:7 + tm, :]
    xs = p + (prev - p) * mu_ref[...]
    buf_scr[7:8, :] = p[tm - 1:tm, :]

    r = xs[:, C_R:C_R + D]
    k = xs[:, C_K:C_K + D]
    v = xs[:, C_V:C_V + D]
    xw = xs[:, C_XW:C_XW + 128]
    xa = xs[:, C_XA:C_XA + 128]
    xg = xs[:, C_XG:C_XG + 256]

    lw = _dot(jnp.tanh(xw).astype(BF16), wd2_ref[...])
    z = -(w0_ref[...] + lw)
    softplus = jnp.maximum(z, 0.0) + jnp.log(1.0 + jnp.exp(-jnp.abs(z)))
    decay = jnp.exp(-jnp.exp(-softplus - 0.5))
    a = _sigmoid(a0_ref[...] + _dot(xa.astype(BF16), wa2_ref[...]))
    g_o[...] = _dot(_sigmoid(xg).astype(BF16), wg2_ref[...])

    seg = seg_ref[...]
    exp = exp_ref[...]
    kk = k * kk_ref[...]
    kk = kk / jnp.maximum(jnp.sqrt(_head_sum(kk * kk, seg, exp)), 1e-12)
    k2 = k * (1.0 + (a - 1.0) * ka_ref[...])
    bo_o[...] = _head_sum(r * k2 * rk_ref[...], seg, exp) * v
    al_o[...] = -kk
    be_o[...] = kk * a
    k_o[...] = k2
    r_o[...] = r
    w_o[...] = decay
    v_o[...] = v

    glu = proj_ref[:, C_CVA:C_CVA + D] * _sigmoid(proj_ref[:, C_CVB:C_CVB + D])
    ext_scr[32:32 + tm, :] = glu
    acc = jnp.zeros((tm, D), F32) + dwb_ref[...]
    for j in range(CONV_W):
        acc = acc + ext_scr[2 + j:2 + j + tm, :] * dww_ref[j:j + 1, :]
    tail = ext_scr[tm:tm + 32, :]
    cv_o[...] = tail[2:32, :]
    ext_scr[0:32, :] = tail
    mean = jnp.mean(acc, axis=-1, keepdims=True)
    cen = acc - mean
    var = jnp.mean(cen * cen, axis=-1, keepdims=True)
    ln = cen * lax.rsqrt(var + LN_EPS) * lng_ref[...] + lnb_ref[...]
    u = (ln * _sigmoid(ln)).astype(BF16)
    yb = _dot(u, wco_ref[...])
    sga_o[...] = _sigmoid(proj_ref[:, C_GA:C_GA + D])
    yb_o[...] = _sigmoid(proj_ref[:, C_GB:C_GB + D]) * yb


def _prep_call(proj, shift0_p, conv0, pw, bn, t, tm):
    nt = t // tm
    row = lambda shape: pl.BlockSpec(shape, lambda b, i: (0,) * len(shape))
    tok = pl.BlockSpec((None, tm, D), lambda b, i: (b, i, 0))
    ins = [pl.BlockSpec((None, tm, NIN), lambda b, i: (b, i, 0)),
           pl.BlockSpec((None, 1, NRW), lambda b, i: (b, 0, 0)),
           pl.BlockSpec((None, CONV_W - 1, D), lambda b, i: (b, 0, 0)),
           row((1, NRW)), row((1, D)), row((1, D)), row((1, D)), row((1, D)), row((1, D)),
           row((128, D)), row((128, D)), row((256, D)),
           row((CONV_W, D)), row((1, D)), row((1, D)), row((1, D)), row((D, D)),
           row((D, LANES)), row((LANES, D))]
    outs = [tok] * 10 + [pl.BlockSpec((None, CONV_W - 1, D), lambda b, i: (b, 0, 0))]
    out_shape = [jax.ShapeDtypeStruct((bn, t, D), F32)] * 10 + [
        jax.ShapeDtypeStruct((bn, CONV_W - 1, D), F32)]
    return pl.pallas_call(
        functools.partial(_prep_body, tm=tm),
        grid=(bn, nt),
        in_specs=ins,
        out_specs=outs,
        out_shape=out_shape,
        scratch_shapes=[pltpu.VMEM((tm + 8, NRW), F32), pltpu.VMEM((tm + 32, D), F32)],
        compiler_params=_params(("arbitrary", "arbitrary")),
        name="rwkv_prep_conv",
    )(proj, shift0_p, conv0, pw["mu"], pw["w0"], pw["a0"], pw["k_k"], pw["k_a"], pw["r_k"],
      pw["wd2"], pw["wa2"], pw["wg2"], pw["dw_w"], pw["dw_b"], pw["ln_g"], pw["ln_b"],
      pw["w_conv_out"], pw["seg"], pw["exp"])


def _wkv_body(al_ref, be_ref, k_ref, r_ref, w_ref, vh_ref, s0_ref, y_ref, so_ref, s_scr, mv_scr,
              *, tb):
    ti = pl.program_id(1)

    @pl.when(ti == 0)
    def _():
        s_scr[...] = s0_ref[...]

    row_x = lax.broadcasted_iota(jnp.int32, (8, GL), 0)
    lane_x = lax.broadcasted_iota(jnp.int32, (8, GL), 1)
    head_mask = (lane_x // N == row_x % HG).astype(F32)
    low_x = row_x < HG
    low_o = lax.broadcasted_iota(jnp.int32, (8, N), 0) < HG

    def matvec(b, g, t_a, t_r):
        lanes = pl.ds(g * GL, GL)
        a_b = jnp.broadcast_to(al_ref[b, pl.ds(t_a, 1), lanes], (8, GL))
        r_b = jnp.broadcast_to(r_ref[b, pl.ds(t_r, 1), lanes], (8, GL))
        x = (jnp.where(low_x, r_b, a_b) if g % 2 == 0 else jnp.where(low_x, a_b, r_b)) * head_mask
        s = s_scr[b, g]
        return lax.dot_general(x.astype(BF16), s.astype(BF16), (((1,), (1,)), ((), ())),
                               preferred_element_type=F32)

    groups = [(b, g) for b in range(WKV_SEQS) for g in range(NG)]
    n_grp = len(groups)
    lag = n_grp // 2

    def matvec_y(i, t_a, t_r):
        b, g = groups[i]
        out = matvec(b, g, t_a, t_r)
        mv_scr[b, g] = out
        if g % 2 == 1:
            j = g // 2
            y_ref[b, t_r, 8 * j:8 * j + 8, :] = jnp.where(low_o, mv_scr[b, g - 1], out)

    def update(i, t):
        b, g = groups[i]
        lanes = pl.ds(g * GL, GL)
        j = g // 2
        vt = vh_ref[b, t, 8 * j:8 * j + 8, :]
        out = mv_scr[b, g]
        lt = jnp.where(low_o, vt, out) if g % 2 == 0 else jnp.where(low_o, out, vt)
        k_b = jnp.broadcast_to(k_ref[b, pl.ds(t, 1), lanes], (8, GL))
        be_b = jnp.broadcast_to(be_ref[b, pl.ds(t, 1), lanes], (8, GL))
        q = (jnp.where(low_x, k_b, be_b) if g % 2 == 0
             else jnp.where(low_x, be_b, k_b)) * head_mask
        ds = lax.dot_general(lt.astype(BF16), q.astype(BF16), (((0,), (0,)), ((), ())),
                             preferred_element_type=F32)
        s_scr[b, g] = s_scr[b, g] * w_ref[b, pl.ds(t, 1), lanes] + ds

    for i in range(n_grp):
        matvec_y(i, 0, 0)
        if i >= lag:
            update(i - lag, 0)

    def step(t, carry):
        for i in range(n_grp):
            matvec_y(i, t, t - 1)
            if i < lag:
                update(i + lag, t - 1)
            else:
                update(i - lag, t)
        return carry

    lax.fori_loop(1, tb, step, 0, unroll=WKV_UNROLL)

    for i in range(lag, n_grp):
        update(i, tb - 1)
    for i in range(n_grp):
        matvec_y(i, tb - 1, tb - 1)

    @pl.when(ti == pl.num_programs(1) - 1)
    def _():
        so_ref[...] = s_scr[...]


def _wkv_call(al, be, k2, r, w, vh, s0, bn, t, tb):
    seq = pl.BlockSpec((WKV_SEQS, tb, D), lambda i, j: (i, j, 0))
    hv = pl.BlockSpec((WKV_SEQS, tb, H, N), lambda i, j: (i, j, 0, 0))
    st = pl.BlockSpec((WKV_SEQS, NG, N, GL), lambda i, j: (i, 0, 0, 0))
    return pl.pallas_call(
        functools.partial(_wkv_body, tb=tb),
        grid=(bn // WKV_SEQS, t // tb),
        in_specs=[seq, seq, seq, seq, seq, hv, st],
        out_specs=[hv, st],
        out_shape=[jax.ShapeDtypeStruct((bn, t, H, N), F32),
                   jax.ShapeDtypeStruct((bn, NG, N, GL), F32)],
        scratch_shapes=[pltpu.VMEM((WKV_SEQS, NG, N, GL), F32),
                        pltpu.VMEM((WKV_SEQS, NG, 8, N), F32)],
        compiler_params=_params(("arbitrary", "arbitrary")),
        name="wkv7_scan",
    )(al, be, k2, r, w, vh, s0)


def _post_body(y_ref, bo_ref, g_ref, sga_ref, yb_ref, x_ref, gtm_ref, scf_ref, shf_ref,
               gng_ref, gnb_ref, wo_ref, wout_ref, postg_ref, preg_ref,
               wrh_ref, wrl_ref, rb_ref, seg_ref, exp_ref,
               x1_o, hf_o, gates_o):
    seg = seg_ref[...]
    exp = exp_ref[...]
    y = y_ref[...]
    mu = _head_sum(y, seg, exp) * (1.0 / N)
    cen = y - mu
    var = _head_sum(cen * cen, seg, exp) * (1.0 / N)
    yn = cen * lax.rsqrt(var + GN_EPS) * gng_ref[...] + gnb_ref[...] + bo_ref[...]
    ya = _dot((yn * g_ref[...]).astype(BF16), wo_ref[...])
    merged = sga_ref[...] * ya + yb_ref[...]
    z = _dot(merged.astype(BF16), wout_ref[...])
    zn = z * lax.rsqrt(jnp.mean(z * z, axis=-1, keepdims=True) + RMS_EPS) * postg_ref[...]
    x1 = x_ref[...] + gtm_ref[...] * zn
    x1_o[...] = x1
    hn = x1 * lax.rsqrt(jnp.mean(x1 * x1, axis=-1, keepdims=True) + RMS_EPS) * preg_ref[...]
    hf = hn * (1.0 + scf_ref[...]) + shf_ref[...]
    hf_o[...] = hf.astype(BF16)

    hh, hl = _split_bf16(hf)
    logits = _dot(hh, wrh_ref[...]) + _dot(hl, wrh_ref[...]) + _dot(hh, wrl_ref[...])
    scores = _sigmoid(logits)
    lane = lax.broadcasted_iota(jnp.int32, scores.shape, 1).astype(F32)
    neg = jnp.float32(-jnp.inf)
    sb = jnp.where(lane < E, scores + rb_ref[...], neg)
    picked = jnp.zeros(scores.shape, F32)
    for _ in range(TOP_K):
        m = jnp.max(sb, axis=-1, keepdims=True)
        first = jnp.min(jnp.where(sb == m, lane, float(LANES)), axis=-1, keepdims=True)
        pick = lane == first
        picked = jnp.where(pick, 1.0, picked)
        sb = jnp.where(pick, neg, sb)
    sel = scores * picked
    gates_o[...] = sel / jnp.sum(sel, axis=-1, keepdims=True) * ROUTED_SCALE


def _post_call(y, bo, g, sga, yb, x, gtm, scf, shf, pw, per_token, tm, rows_per_seq):
    ntok = x.shape[0]
    tok = pl.BlockSpec((tm, D), lambda i: (i, 0))
    row = lambda shape: pl.BlockSpec(shape, lambda i: (0,) * len(shape))
    mspec = _mod_spec(per_token, tm, rows_per_seq)
    return pl.pallas_call(
        _post_body,
        grid=(ntok // tm,),
        in_specs=[tok] * 6 + [mspec] * 3 + [
            row((1, D)), row((1, D)), row((D, D)), row((D, D)), row((1, D)), row((1, D)),
            row((D, LANES)), row((D, LANES)), row((1, LANES)), row((D, LANES)), row((LANES, D))],
        out_specs=[tok, tok, pl.BlockSpec((tm, LANES), lambda i: (i, 0))],
        out_shape=[jax.ShapeDtypeStruct((ntok, D), F32),
                   jax.ShapeDtypeStruct((ntok, D), BF16),
                   jax.ShapeDtypeStruct((ntok, LANES), F32)],
        compiler_params=_params(("arbitrary",)),
        name="mix_post_router",
    )(y, bo, g, sga, yb, x, gtm, scf, shf, pw["gn_g"], pw["gn_b"], pw["w_o_rwkv"], pw["w_out"],
      pw["mix_post_g"], pw["ffn_pre_g"], pw["wr_hi"], pw["wr_lo"], pw["router_bias"],
      pw["seg"], pw["exp"])


MOE_EB = 4


def _moe_body(hf_ref, gates_ref, x1_ref, gtf_ref, postg_ref, wsg_ref, wsu_ref, wsd_ref,
              wg_ref, wu_ref, wd_ref, o_ref, acc_scr):
    e0 = pl.program_id(1)
    t = hf_ref[...]

    @pl.when(e0 == 0)
    def _():
        hg = _dot(t, wsg_ref[...])
        he = hg * _sigmoid(hg) * _dot(t, wsu_ref[...])
        acc_scr[...] = _dot(he.astype(BF16), wsd_ref[...])

    gates = gates_ref[...]
    lane = lax.broadcasted_iota(jnp.int32, gates.shape, 1)
    acc = acc_scr[...]
    for i in range(MOE_EB):
        gcol = jnp.sum(jnp.where(lane == e0 * MOE_EB + i, gates, 0.0), axis=-1, keepdims=True)
        hg = _dot(t, wg_ref[i])
        he = hg * _sigmoid(hg) * _dot(t, wu_ref[i])
        acc = acc + gcol * _dot(he.astype(BF16), wd_ref[i])
    acc_scr[...] = acc

    @pl.when(e0 == pl.num_programs(1) - 1)
    def _():
        out = acc_scr[...]
        on = out * lax.rsqrt(jnp.mean(out * out, axis=-1, keepdims=True) + RMS_EPS) * postg_ref[...]
        o_ref[...] = x1_ref[...] + gtf_ref[...] * on


def _moe_call(hf, gates, x1, gtf, pw, per_token, tm, rows_per_seq):
    ntok = hf.shape[0]
    tok = pl.BlockSpec((tm, D), lambda i, e: (i, 0))
    row = lambda shape: pl.BlockSpec(shape, lambda i, e: (0,) * len(shape))
    mspec = _mod_spec(per_token, tm, rows_per_seq)
    return pl.pallas_call(
        _moe_body,
        grid=(ntok // tm, E // MOE_EB),
        in_specs=[tok, pl.BlockSpec((tm, LANES), lambda i, e: (i, 0)), tok, mspec,
                  row((1, D)), row((D, DE)), row((D, DE)), row((DE, D)),
                  pl.BlockSpec((MOE_EB, D, DE), lambda i, e: (e, 0, 0)),
                  pl.BlockSpec((MOE_EB, D, DE), lambda i, e: (e, 0, 0)),
                  pl.BlockSpec((MOE_EB, DE, D), lambda i, e: (e, 0, 0))],
        out_specs=tok,
        out_shape=jax.ShapeDtypeStruct((ntok, D), F32),
        scratch_shapes=[pltpu.VMEM((tm, D), F32)],
        compiler_params=_params(("arbitrary", "arbitrary")),
        name="moe_ffn",
    )(hf, gates, x1, gtf, pw["ffn_post_g"], pw["w_sh_gate"], pw["w_sh_up"], pw["w_sh_down"],
      pw["w_exp_gate"], pw["w_exp_up"], pw["w_exp_down"])


def _pad_rwkv_cols(a):
    pad = lambda z, n: jnp.pad(z, [(0, 0)] * (z.ndim - 1) + [(0, n - z.shape[-1])])
    return jnp.concatenate([a[..., :3072], pad(a[..., 3072:3136], 128), pad(a[..., 3136:3200], 128),
                            pad(a[..., 3200:3360], 256)], axis=-1)


def _unpad_rwkv_cols(a):
    return jnp.concatenate([a[..., :3072], a[..., C_XW:C_XW + 64], a[..., C_XA:C_XA + 64],
                            a[..., C_XG:C_XG + 160]], axis=-1)


def _layer(x, mod, wkv0, shift0, conv0, pw, per_token, tiles):
    bn, t, _ = x.shape
    ntok = bn * t
    xf = x.reshape(ntok, D)
    chunks = [mod[:, i * D:(i + 1) * D] for i in range(6)]
    if per_token:
        chunks = [jnp.repeat(c, t, axis=0).reshape(1, ntok, D) for c in chunks]
    else:
        chunks = [c.reshape(bn, 1, D) for c in chunks]
    sh_m, sc_m, gt_m, sh_f, sc_f, gt_f = chunks

    proj = _proj_call(xf, sc_m, sh_m, pw["mix_pre_g"], pw["w_in"], per_token, tiles["proj"], t)
    proj = proj.reshape(bn, t, NIN)
    shift_out = _unpad_rwkv_cols(proj[:, t - 1, :NRW])

    shift0_p = _pad_rwkv_cols(shift0).reshape(bn, 1, NRW)
    al, be, k2, r, w, v, g, bo, sga, yb, conv_out = _prep_call(
        proj, shift0_p, conv0, pw, bn, t, tiles["prep"])

    flat = lambda a: a.reshape(ntok, D)
    vh = v.reshape(bn, t, H, N)
    s0 = wkv0.reshape(bn, NG, HG, N, N).transpose(0, 1, 3, 2, 4).reshape(bn, NG, N, GL)
    yh, s_out = _wkv_call(al, be, k2, r, w, vh, s0, bn, t, tiles["wkv"])
    wkv_out = s_out.reshape(bn, NG, N, HG, N).transpose(0, 1, 3, 2, 4).reshape(bn, H, N, N)
    y = yh.reshape(ntok, D)

    x1, hf, gates = _post_call(y, flat(bo), flat(g), flat(sga), flat(yb), xf, gt_m, sc_f, sh_f,
                               pw, per_token, tiles["post"], t)
    x2 = _moe_call(hf, gates, x1, gt_f, pw, per_token, tiles["moe"], t)
    return x2.reshape(bn, t, D), wkv_out, shift_out, conv_out


def kernel(x_prompt, x_sample, state_wkv, state_shift, state_conv, c_prompt, c_sample, w_ada, b_ada, mix_pre_g, mix_post_g, w_in, mu_shift, w0, w_decay2, a0, w_aaa2, w_gate2, k_k, k_a, r_k, gn_g, gn_b, w_o_rwkv, dw_w, dw_b, conv_ln_g, conv_ln_b, w_conv_out, w_out, ffn_pre_g, ffn_post_g, w_router, router_bias, w_exp_gate, w_exp_up, w_exp_down, w_sh_gate, w_sh_up, w_sh_down):
    bp = x_prompt.shape[0]
    bs = x_sample.shape[0]
    row = lambda a: a.reshape(1, -1).astype(F32)
    padr = lambda a, n: jnp.pad(a, ((0, n - a.shape[0]), (0, 0)))
    padc = lambda a, n: jnp.pad(a, ((0, 0), (0, n - a.shape[1])))

    w_in_p = jnp.concatenate(
        [w_in[:, :3072], padc(w_in[:, 3072:3136], 128), padc(w_in[:, 3136:3200], 128),
         padc(w_in[:, 3200:3360], 256), w_in[:, 3360:]], axis=1).astype(BF16)
    head_of_col = jnp.arange(D, dtype=jnp.int32) // N
    seg = (head_of_col[:, None] == jnp.arange(LANES, dtype=jnp.int32)[None, :]).astype(BF16)
    wr = padc(w_router, LANES)
    wr_hi = wr.astype(BF16)
    pw = {
        "w_in": w_in_p,
        "mix_pre_g": row(mix_pre_g), "mix_post_g": row(mix_post_g),
        "ffn_pre_g": row(ffn_pre_g), "ffn_post_g": row(ffn_post_g),
        "mu": row(_pad_rwkv_cols(mu_shift)), "w0": row(w0), "a0": row(a0),
        "k_k": row(k_k), "k_a": row(k_a), "r_k": row(r_k),
        "wd2": padr(w_decay2, 128).astype(BF16), "wa2": padr(w_aaa2, 128).astype(BF16),
        "wg2": padr(w_gate2, 256).astype(BF16),
        "dw_w": dw_w, "dw_b": row(dw_b), "ln_g": row(conv_ln_g), "ln_b": row(conv_ln_b),
        "w_conv_out": w_conv_out.astype(BF16),
        "gn_g": row(gn_g), "gn_b": row(gn_b),
        "w_o_rwkv": w_o_rwkv.astype(BF16), "w_out": w_out.astype(BF16),
        "wr_hi": wr_hi, "wr_lo": (wr - wr_hi.astype(F32)).astype(BF16),
        "router_bias": padc(row(router_bias), LANES),
        "seg": seg, "exp": seg.T,
        "w_sh_gate": w_sh_gate.astype(BF16), "w_sh_up": w_sh_up.astype(BF16),
        "w_sh_down": w_sh_down.astype(BF16),
        "w_exp_gate": w_exp_gate.astype(BF16), "w_exp_up": w_exp_up.astype(BF16),
        "w_exp_down": w_exp_down.astype(BF16),
    }

    mod = _mod_call(jnp.concatenate([c_prompt, c_sample], axis=0), w_ada, b_ada)
    zeros = lambda shape: jnp.zeros(shape, x_prompt.dtype)
    yp, wkv_p, shift_p, conv_p = _layer(
        x_prompt, mod[:bp], zeros((bp, H, N, N)), zeros((bp, state_shift.shape[1])),
        zeros((bp, CONV_W - 1, D)), pw, per_token=False,
        tiles={"proj": 1024, "prep": 256, "wkv": 64, "post": 256, "moe": 1024})
    ts = x_sample.shape[1]
    ys, wkv_s, shift_s, conv_s = _layer(
        x_sample, mod[bp:], state_wkv, state_shift, state_conv, pw, per_token=True,
        tiles={"proj": bs * ts, "prep": ts, "wkv": ts, "post": 256, "moe": bs * ts})
    return (yp, ys, wkv_p, shift_p, conv_p, wkv_s, shift_s, conv_s)
```

```python
import functools

import jax
import jax.numpy as jnp
from jax import lax
from jax.experimental import pallas as pl
from jax.experimental.pallas import tpu as pltpu

F32 = jnp.float32
BF16 = jnp.bfloat16

D = 1024
H = 16
N = 64
E = 64
DE = 256
TOP_K = 6
ROUTED_SCALE = 2.5
CONV_W = 31
RMS_EPS = 1e-6
LN_EPS = 1e-5
GN_EPS = N * 1e-5

C_R, C_K, C_V = 0, 1024, 2048
C_XW, C_XA, C_XG = 3072, 3200, 3328
NRW = 3584
C_CVA, C_CVB, C_GA, C_GB = 3584, 4608, 5632, 6656
NIN = 7680

LANES = 128
HG = 4
GL = HG * N
NG = H // HG
WKV_SEQS = 8
WKV_UNROLL = 4
VMEM_LIMIT = 56 * 1024 * 1024


def _dot(a, b):
    return jnp.dot(a, b, preferred_element_type=F32)


def _sigmoid(x):
    return jax.nn.sigmoid(x)


def _split_bf16(x):
    hi = x.astype(BF16)
    lo = (x - hi.astype(F32)).astype(BF16)
    return hi, lo


def _head_sum(x, seg, exp):
    hi, lo = _split_bf16(x)
    s = _dot(hi, seg) + _dot(lo, seg)
    shi, slo = _split_bf16(s)
    return _dot(shi, exp) + _dot(slo, exp)


def _params(sem, vmem=VMEM_LIMIT):
    return pltpu.CompilerParams(dimension_semantics=sem, vmem_limit_bytes=vmem)


def _mod_body(c_ref, w_ref, b_ref, o_ref):
    c = c_ref[...]
    s = (c * _sigmoid(c)).astype(BF16)
    o_ref[...] = _dot(s, w_ref[...].astype(BF16)) + b_ref[...]


def _mod_call(c_all, w_ada, b_ada):
    rows = c_all.shape[0]
    tn = 512
    return pl.pallas_call(
        _mod_body,
        grid=(6 * D // tn,),
        in_specs=[pl.BlockSpec((rows, D), lambda j: (0, 0)),
                  pl.BlockSpec((D, tn), lambda j: (0, j)),
                  pl.BlockSpec((1, tn), lambda j: (0, j))],
        out_specs=pl.BlockSpec((rows, tn), lambda j: (0, j)),
        out_shape=jax.ShapeDtypeStruct((rows, 6 * D), F32),
        compiler_params=_params(("arbitrary",)),
        name="ada_mod",
    )(c_all, w_ada, b_ada.reshape(1, 6 * D))


def _mod_spec(per_token, tm, rows_per_seq):
    if per_token:
        return pl.BlockSpec((None, tm, D), lambda i, *_: (0, i, 0))
    return pl.BlockSpec((None, 1, D), lambda i, *_: (i * tm // rows_per_seq, 0, 0))


def _proj_body(x_ref, sc_ref, sh_ref, g_ref, w_ref, o_ref, h_scr):
    @pl.when(pl.program_id(1) == 0)
    def _():
        x = x_ref[...]
        y = x * lax.rsqrt(jnp.mean(x * x, axis=-1, keepdims=True) + RMS_EPS) * g_ref[...]
        h_scr[...] = (y * (1.0 + sc_ref[...]) + sh_ref[...]).astype(BF16)

    o_ref[...] = _dot(h_scr[...], w_ref[...])


def _proj_call(x, sc, sh, g, w_in_p, per_token, tm, rows_per_seq):
    ntok = x.shape[0]
    tn = 1280
    mspec = _mod_spec(per_token, tm, rows_per_seq)
    return pl.pallas_call(
        _proj_body,
        grid=(ntok // tm, NIN // tn),
        in_specs=[pl.BlockSpec((tm, D), lambda i, j: (i, 0)),
                  mspec, mspec,
                  pl.BlockSpec((1, D), lambda i, j: (0, 0)),
                  pl.BlockSpec((D, tn), lambda i, j: (0, j))],
        out_specs=pl.BlockSpec((tm, tn), lambda i, j: (i, j)),
        out_shape=jax.ShapeDtypeStruct((ntok, NIN), F32),
        scratch_shapes=[pltpu.VMEM((tm, D), BF16)],
        compiler_params=_params(("arbitrary", "arbitrary")),
        name="in_proj",
    )(x, sc, sh, g, w_in_p)


def _prep_body(proj_ref, sh0_ref, cv0_ref, mu_ref, w0_ref, a0_ref, kk_ref, ka_ref, rk_ref,
               wd2_ref, wa2_ref, wg2_ref, dww_ref, dwb_ref, lng_ref, lnb_ref, wco_ref,
               seg_ref, exp_ref,
               al_o, be_o, k_o, r_o, w_o, v_o, g_o, bo_o, sga_o, yb_o, cv_o,
               buf_scr, ext_scr, part_scr, *, tm):
    ti = pl.program_id(1)

    @pl.when(ti == 0)
    def _():
        buf_scr[7:8, :] = sh0_ref[...]
        ext_scr[0:2, :] = jnp.zeros((2, D), F32)
        ext_scr[2:32, :] = cv0_ref[...]

    p = proj_ref[:, 0:NRW]
    buf_scr[8:8 + tm, :] = p
    prev = buf_scr[7:7 + tm, :]
    xs = p + (prev - p) * mu_ref[...]
    buf_scr[7:8, :] = p[tm - 1:tm, :]

    r = xs[:, C_R:C_R + D]
    k = xs[:, C_K:C_K + D]
    v = xs[:, C_V:C_V + D]
    xw = xs[:, C_XW:C_XW + 128]
    xa = xs[:, C_XA:C_XA + 128]
    xg = xs[:, C_XG:C_XG + 256]

    lw = _dot(jnp.tanh(xw).astype(BF16), wd2_ref[...])
    z = -(w0_ref[...] + lw)
    softplus = jnp.maximum(z, 0.0) + jnp.log(1.0 + jnp.exp(-jnp.abs(z)))
    decay = jnp.exp(-jnp.exp(-softplus - 0.5))
    a = _sigmoid(a0_ref[...] + _dot(xa.astype(BF16), wa2_ref[...]))
    g_o[...] = _dot(_sigmoid(xg).astype(BF16), wg2_ref[...])

    seg = seg_ref[...]
    exp = exp_ref[...]
    kk = k * kk_ref[...]
    kk = kk / jnp.maximum(jnp.sqrt(_head_sum(kk * kk, seg, exp)), 1e-12)
    k2 = k * (1.0 + (a - 1.0) * ka_ref[...])
    bo_o[...] = _head_sum(r * k2 * rk_ref[...], seg, exp) * v
    al_o[...] = -kk
    be_o[...] = kk * a
    k_o[...] = k2
    r_o[...] = r
    w_o[...] = decay
    v_o[...] = v

    glu = proj_ref[:, C_CVA:C_CVA + D] * _sigmoid(proj_ref[:, C_CVB:C_CVB + D])
    ext_scr[32:32 + tm, :] = glu
    acc = jnp.zeros((tm, D), F32) + dwb_ref[...]
    for s in range(8):
        rows = tm if s == 0 else tm + 8
        part = None
        for q in range(5):
            o = 8 * q + s
            if 2 <= o <= CONV_W + 1:
                term = ext_scr[8 * q:8 * q + rows, :] * dww_ref[o - 2:o - 1, :]
                part = term if part is None else part + term
        if s == 0:
            acc = acc + part
        else:
            part_scr[...] = part
            acc = acc + part_scr[s:s + tm, :]
    tail = ext_scr[tm:tm + 32, :]
    cv_o[...] = tail[2:32, :]
    ext_scr[0:32, :] = tail
    mean = jnp.mean(acc, axis=-1, keepdims=True)
    cen = acc - mean
    var = jnp.mean(cen * cen, axis=-1, keepdims=True)
    ln = cen * lax.rsqrt(var + LN_EPS) * lng_ref[...] + lnb_ref[...]
    u = (ln * _sigmoid(ln)).astype(BF16)
    yb = _dot(u, wco_ref[...])
    sga_o[...] = _sigmoid(proj_ref[:, C_GA:C_GA + D])
    yb_o[...] = _sigmoid(proj_ref[:, C_GB:C_GB + D]) * yb


def _prep_call(proj, shift0_p, conv0, pw, bn, t, tm):
    nt = t // tm
    row = lambda shape: pl.BlockSpec(shape, lambda b, i: (0,) * len(shape))
    tok = pl.BlockSpec((None, tm, D), lambda b, i: (b, i, 0))
    ins = [pl.BlockSpec((None, tm, NIN), lambda b, i: (b, i, 0)),
           pl.BlockSpec((None, 1, NRW), lambda b, i: (b, 0, 0)),
           pl.BlockSpec((None, CONV_W - 1, D), lambda b, i: (b, 0, 0)),
           row((1, NRW)), row((1, D)), row((1, D)), row((1, D)), row((1, D)), row((1, D)),
           row((128, D)), row((128, D)), row((256, D)),
           row((CONV_W, D)), row((1, D)), row((1, D)), row((1, D)), row((D, D)),
           row((D, LANES)), row((LANES, D))]
    outs = [tok] * 10 + [pl.BlockSpec((None, CONV_W - 1, D), lambda b, i: (b, 0, 0))]
    out_shape = [jax.ShapeDtypeStruct((bn, t, D), F32)] * 10 + [
        jax.ShapeDtypeStruct((bn, CONV_W - 1, D), F32)]
    return pl.pallas_call(
        functools.partial(_prep_body, tm=tm),
        grid=(bn, nt),
        in_specs=ins,
        out_specs=outs,
        out_shape=out_shape,
        scratch_shapes=[pltpu.VMEM((tm + 8, NRW), F32), pltpu.VMEM((tm + 32, D), F32),
                        pltpu.VMEM((tm + 8, D), F32)],
        compiler_params=_params(("arbitrary", "arbitrary")),
        name="rwkv_prep_conv",
    )(proj, shift0_p, conv0, pw["mu"], pw["w0"], pw["a0"], pw["k_k"], pw["k_a"], pw["r_k"],
      pw["wd2"], pw["wa2"], pw["wg2"], pw["dw_w"], pw["dw_b"], pw["ln_g"], pw["ln_b"],
      pw["w_conv_out"], pw["seg"], pw["exp"])


def _wkv_body(al_ref, be_ref, k_ref, r_ref, w_ref, vh_ref, s0_ref, y_ref, so_ref, s_scr, mv_scr,
              *, tb):
    ti = pl.program_id(1)

    @pl.when(ti == 0)
    def _():
        s_scr[...] = s0_ref[...]

    row_x = lax.broadcasted_iota(jnp.int32, (8, GL), 0)
    lane_x = lax.broadcasted_iota(jnp.int32, (8, GL), 1)
    head_mask = (lane_x // N == row_x % HG).astype(F32)
    low_x = row_x < HG
    low_o = lax.broadcasted_iota(jnp.int32, (8, N), 0) < HG

    def matvec(b, g, t_a, t_r):
        lanes = pl.ds(g * GL, GL)
        a_b = jnp.broadcast_to(al_ref[b, pl.ds(t_a, 1), lanes], (8, GL))
        r_b = jnp.broadcast_to(r_ref[b, pl.ds(t_r, 1), lanes], (8, GL))
        x = (jnp.where(low_x, r_b, a_b) if g % 2 == 0 else jnp.where(low_x, a_b, r_b)) * head_mask
        s = s_scr[b, g]
        return lax.dot_general(x.astype(BF16), s.astype(BF16), (((1,), (1,)), ((), ())),
                               preferred_element_type=F32)

    groups = [(b, g) for b in range(WKV_SEQS) for g in range(NG)]
    n_grp = len(groups)
    lag = n_grp // 2

    def matvec_y(i, t_a, t_r):
        b, g = groups[i]
        out = matvec(b, g, t_a, t_r)
        mv_scr[b, g] = out
        if g % 2 == 1:
            j = g // 2
            y_ref[b, t_r, 8 * j:8 * j + 8, :] = jnp.where(low_o, mv_scr[b, g - 1], out)

    def update(i, t):
        b, g = groups[i]
        lanes = pl.ds(g * GL, GL)
        j = g // 2
        vt = vh_ref[b, t, 8 * j:8 * j + 8, :]
        out = mv_scr[b, g]
        lt = jnp.where(low_o, vt, out) if g % 2 == 0 else jnp.where(low_o, out, vt)
        k_b = jnp.broadcast_to(k_ref[b, pl.ds(t, 1), lanes], (8, GL))
        be_b = jnp.broadcast_to(be_ref[b, pl.ds(t, 1), lanes], (8, GL))
        q = (jnp.where(low_x, k_b, be_b) if g % 2 == 0
             else jnp.where(low_x, be_b, k_b)) * head_mask
        ds = lax.dot_general(lt.astype(BF16), q.astype(BF16), (((0,), (0,)), ((), ())),
                             preferred_element_type=F32)
        s_scr[b, g] = s_scr[b, g] * w_ref[b, pl.ds(t, 1), lanes] + ds

    for i in range(n_grp):
        matvec_y(i, 0, 0)
        if i >= lag:
            update(i - lag, 0)

    def step(t, carry):
        for i in range(n_grp):
            matvec_y(i, t, t - 1)
            if i < lag:
                update(i + lag, t - 1)
            else:
                update(i - lag, t)
        return carry

    lax.fori_loop(1, tb, step, 0, unroll=WKV_UNROLL)

    for i in range(lag, n_grp):
        update(i, tb - 1)
    for i in range(n_grp):
        matvec_y(i, tb - 1, tb - 1)

    @pl.when(ti == pl.num_programs(1) - 1)
    def _():
        so_ref[...] = s_scr[...]


def _wkv_call(al, be, k2, r, w, vh, s0, bn, t, tb):
    seq = pl.BlockSpec((WKV_SEQS, tb, D), lambda i, j: (i, j, 0))
    hv = pl.BlockSpec((WKV_SEQS, tb, H, N), lambda i, j: (i, j, 0, 0))
    st = pl.BlockSpec((WKV_SEQS, NG, N, GL), lambda i, j: (i, 0, 0, 0))
    return pl.pallas_call(
        functools.partial(_wkv_body, tb=tb),
        grid=(bn // WKV_SEQS, t // tb),
        in_specs=[seq, seq, seq, seq, seq, hv, st],
        out_specs=[hv, st],
        out_shape=[jax.ShapeDtypeStruct((bn, t, H, N), F32),
                   jax.ShapeDtypeStruct((bn, NG, N, GL), F32)],
        scratch_shapes=[pltpu.VMEM((WKV_SEQS, NG, N, GL), F32),
                        pltpu.VMEM((WKV_SEQS, NG, 8, N), F32)],
        compiler_params=_params(("arbitrary", "arbitrary")),
        name="wkv7_scan",
    )(al, be, k2, r, w, vh, s0)


def _post_body(y_ref, bo_ref, g_ref, sga_ref, yb_ref, x_ref, gtm_ref, scf_ref, shf_ref,
               gng_ref, gnb_ref, wo_ref, wout_ref, postg_ref, preg_ref,
               wrh_ref, wrl_ref, rb_ref, seg_ref, exp_ref,
               x1_o, hf_o, gates_o):
    seg = seg_ref[...]
    exp = exp_ref[...]
    y = y_ref[...]
    mu = _head_sum(y, seg, exp) * (1.0 / N)
    cen = y - mu
    var = _head_sum(cen * cen, seg, exp) * (1.0 / N)
    yn = cen * lax.rsqrt(var + GN_EPS) * gng_ref[...] + gnb_ref[...] + bo_ref[...]
    ya = _dot((yn * g_ref[...]).astype(BF16), wo_ref[...])
    merged = sga_ref[...] * ya + yb_ref[...]
    z = _dot(merged.astype(BF16), wout_ref[...])
    zn = z * lax.rsqrt(jnp.mean(z * z, axis=-1, keepdims=True) + RMS_EPS) * postg_ref[...]
    x1 = x_ref[...] + gtm_ref[...] * zn
    x1_o[...] = x1
    hn = x1 * lax.rsqrt(jnp.mean(x1 * x1, axis=-1, keepdims=True) + RMS_EPS) * preg_ref[...]
    hf = hn * (1.0 + scf_ref[...]) + shf_ref[...]
    hf_o[...] = hf.astype(BF16)

    hh, hl = _split_bf16(hf)
    logits = _dot(hh, wrh_ref[...]) + _dot(hl, wrh_ref[...]) + _dot(hh, wrl_ref[...])
    scores = _sigmoid(logits)
    lane = lax.broadcasted_iota(jnp.int32, scores.shape, 1).astype(F32)
    neg = jnp.float32(-jnp.inf)
    sb = jnp.where(lane < E, scores + rb_ref[...], neg)
    picked = jnp.zeros(scores.shape, F32)
    for _ in range(TOP_K):
        m = jnp.max(sb, axis=-1, keepdims=True)
        first = jnp.min(jnp.where(sb == m, lane, float(LANES)), axis=-1, keepdims=True)
        pick = lane == first
        picked = jnp.where(pick, 1.0, picked)
        sb = jnp.where(pick, neg, sb)
    sel = scores * picked
    gates_o[...] = sel / jnp.sum(sel, axis=-1, keepdims=True) * ROUTED_SCALE


def _post_call(y, bo, g, sga, yb, x, gtm, scf, shf, pw, per_token, tm, rows_per_seq):
    ntok = x.shape[0]
    tok = pl.BlockSpec((tm, D), lambda i: (i, 0))
    row = lambda shape: pl.BlockSpec(shape, lambda i: (0,) * len(shape))
    mspec = _mod_spec(per_token, tm, rows_per_seq)
    return pl.pallas_call(
        _post_body,
        grid=(ntok // tm,),
        in_specs=[tok] * 6 + [mspec] * 3 + [
            row((1, D)), row((1, D)), row((D, D)), row((D, D)), row((1, D)), row((1, D)),
            row((D, LANES)), row((D, LANES)), row((1, LANES)), row((D, LANES)), row((LANES, D))],
        out_specs=[tok, tok, pl.BlockSpec((tm, LANES), lambda i: (i, 0))],
        out_shape=[jax.ShapeDtypeStruct((ntok, D), F32),
                   jax.ShapeDtypeStruct((ntok, D), BF16),
                   jax.ShapeDtypeStruct((ntok, LANES), F32)],
        compiler_params=_params(("arbitrary",)),
        name="mix_post_router",
    )(y, bo, g, sga, yb, x, gtm, scf, shf, pw["gn_g"], pw["gn_b"], pw["w_o_rwkv"], pw["w_out"],
      pw["mix_post_g"], pw["ffn_pre_g"], pw["wr_hi"], pw["wr_lo"], pw["router_bias"],
      pw["seg"], pw["exp"])


MOE_EB = 4


def _moe_body(hf_ref, gates_ref, x1_ref, gtf_ref, postg_ref, wsg_ref, wsu_ref, wsd_ref,
              wg_ref, wu_ref, wd_ref, o_ref, acc_scr):
    e0 = pl.program_id(1)
    t = hf_ref[...]

    @pl.when(e0 == 0)
    def _():
        hg = _dot(t, wsg_ref[...])
        he = hg * _sigmoid(hg) * _dot(t, wsu_ref[...])
        acc_scr[...] = _dot(he.astype(BF16), wsd_ref[...])

    gates = gates_ref[...]
    lane = lax.broadcasted_iota(jnp.int32, gates.shape, 1)
    acc = acc_scr[...]
    for i in range(MOE_EB):
        gcol = jnp.sum(jnp.where(lane == e0 * MOE_EB + i, gates, 0.0), axis=-1, keepdims=True)
        hg = _dot(t, wg_ref[i])
        he = hg * _sigmoid(hg) * _dot(t, wu_ref[i])
        acc = acc + gcol * _dot(he.astype(BF16), wd_ref[i])
    acc_scr[...] = acc

    @pl.when(e0 == pl.num_programs(1) - 1)
    def _():
        out = acc_scr[...]
        on = out * lax.rsqrt(jnp.mean(out * out, axis=-1, keepdims=True) + RMS_EPS) * postg_ref[...]
        o_ref[...] = x1_ref[...] + gtf_ref[...] * on


def _moe_call(hf, gates, x1, gtf, pw, per_token, tm, rows_per_seq):
    ntok = hf.shape[0]
    tok = pl.BlockSpec((tm, D), lambda i, e: (i, 0))
    row = lambda shape: pl.BlockSpec(shape, lambda i, e: (0,) * len(shape))
    mspec = _mod_spec(per_token, tm, rows_per_seq)
    return pl.pallas_call(
        _moe_body,
        grid=(ntok // tm, E // MOE_EB),
        in_specs=[tok, pl.BlockSpec((tm, LANES), lambda i, e: (i, 0)), tok, mspec,
                  row((1, D)), row((D, DE)), row((D, DE)), row((DE, D)),
                  pl.BlockSpec((MOE_EB, D, DE), lambda i, e: (e, 0, 0)),
                  pl.BlockSpec((MOE_EB, D, DE), lambda i, e: (e, 0, 0)),
                  pl.BlockSpec((MOE_EB, DE, D), lambda i, e: (e, 0, 0))],
        out_specs=tok,
        out_shape=jax.ShapeDtypeStruct((ntok, D), F32),
        scratch_shapes=[pltpu.VMEM((tm, D), F32)],
        compiler_params=_params(("arbitrary", "arbitrary")),
        name="moe_ffn",
    )(hf, gates, x1, gtf, pw["ffn_post_g"], pw["w_sh_gate"], pw["w_sh_up"], pw["w_sh_down"],
      pw["w_exp_gate"], pw["w_exp_up"], pw["w_exp_down"])


def _pad_rwkv_cols(a):
    pad = lambda z, n: jnp.pad(z, [(0, 0)] * (z.ndim - 1) + [(0, n - z.shape[-1])])
    return jnp.concatenate([a[..., :3072], pad(a[..., 3072:3136], 128), pad(a[..., 3136:3200], 128),
                            pad(a[..., 3200:3360], 256)], axis=-1)


def _unpad_rwkv_cols(a):
    return jnp.concatenate([a[..., :3072], a[..., C_XW:C_XW + 64], a[..., C_XA:C_XA + 64],
                            a[..., C_XG:C_XG + 160]], axis=-1)


def _layer(x, mod, wkv0, shift0, conv0, pw, per_token, tiles):
    bn, t, _ = x.shape
    ntok = bn * t
    xf = x.reshape(ntok, D)
    chunks = [mod[:, i * D:(i + 1) * D] for i in range(6)]
    if per_token:
        chunks = [jnp.repeat(c, t, axis=0).reshape(1, ntok, D) for c in chunks]
    else:
        chunks = [c.reshape(bn, 1, D) for c in chunks]
    sh_m, sc_m, gt_m, sh_f, sc_f, gt_f = chunks

    proj = _proj_call(xf, sc_m, sh_m, pw["mix_pre_g"], pw["w_in"], per_token, tiles["proj"], t)
    proj = proj.reshape(bn, t, NIN)
    shift_out = _unpad_rwkv_cols(proj[:, t - 1, :NRW])

    shift0_p = _pad_rwkv_cols(shift0).reshape(bn, 1, NRW)
    al, be, k2, r, w, v, g, bo, sga, yb, conv_out = _prep_call(
        proj, shift0_p, conv0, pw, bn, t, tiles["prep"])

    flat = lambda a: a.reshape(ntok, D)
    vh = v.reshape(bn, t, H, N)
    s0 = wkv0.reshape(bn, NG, HG, N, N).transpose(0, 1, 3, 2, 4).reshape(bn, NG, N, GL)
    yh, s_out = _wkv_call(al, be, k2, r, w, vh, s0, bn, t, tiles["wkv"])
    wkv_out = s_out.reshape(bn, NG, N, HG, N).transpose(0, 1, 3, 2, 4).reshape(bn, H, N, N)
    y = yh.reshape(ntok, D)

    x1, hf, gates = _post_call(y, flat(bo), flat(g), flat(sga), flat(yb), xf, gt_m, sc_f, sh_f,
                               pw, per_token, tiles["post"], t)
    x2 = _moe_call(hf, gates, x1, gt_f, pw, per_token, tiles["moe"], t)
    return x2.reshape(bn, t, D), wkv_out, shift_out, conv_out


def kernel(x_prompt, x_sample, state_wkv, state_shift, state_conv, c_prompt, c_sample, w_ada, b_ada, mix_pre_g, mix_post_g, w_in, mu_shift, w0, w_decay2, a0, w_aaa2, w_gate2, k_k, k_a, r_k, gn_g, gn_b, w_o_rwkv, dw_w, dw_b, conv_ln_g, conv_ln_b, w_conv_out, w_out, ffn_pre_g, ffn_post_g, w_router, router_bias, w_exp_gate, w_exp_up, w_exp_down, w_sh_gate, w_sh_up, w_sh_down):
    bp = x_prompt.shape[0]
    bs = x_sample.shape[0]
    row = lambda a: a.reshape(1, -1).astype(F32)
    padr = lambda a, n: jnp.pad(a, ((0, n - a.shape[0]), (0, 0)))
    padc = lambda a, n: jnp.pad(a, ((0, 0), (0, n - a.shape[1])))

    w_in_p = jnp.concatenate(
        [w_in[:, :3072], padc(w_in[:, 3072:3136], 128), padc(w_in[:, 3136:3200], 128),
         padc(w_in[:, 3200:3360], 256), w_in[:, 3360:]], axis=1).astype(BF16)
    head_of_col = jnp.arange(D, dtype=jnp.int32) // N
    seg = (head_of_col[:, None] == jnp.arange(LANES, dtype=jnp.int32)[None, :]).astype(BF16)
    wr = padc(w_router, LANES)
    wr_hi = wr.astype(BF16)
    pw = {
        "w_in": w_in_p,
        "mix_pre_g": row(mix_pre_g), "mix_post_g": row(mix_post_g),
        "ffn_pre_g": row(ffn_pre_g), "ffn_post_g": row(ffn_post_g),
        "mu": row(_pad_rwkv_cols(mu_shift)), "w0": row(w0), "a0": row(a0),
        "k_k": row(k_k), "k_a": row(k_a), "r_k": row(r_k),
        "wd2": padr(w_decay2, 128).astype(BF16), "wa2": padr(w_aaa2, 128).astype(BF16),
        "wg2": padr(w_gate2, 256).astype(BF16),
        "dw_w": dw_w, "dw_b": row(dw_b), "ln_g": row(conv_ln_g), "ln_b": row(conv_ln_b),
        "w_conv_out": w_conv_out.astype(BF16),
        "gn_g": row(gn_g), "gn_b": row(gn_b),
        "w_o_rwkv": w_o_rwkv.astype(BF16), "w_out": w_out.astype(BF16),
        "wr_hi": wr_hi, "wr_lo": (wr - wr_hi.astype(F32)).astype(BF16),
        "router_bias": padc(row(router_bias), LANES),
        "seg": seg, "exp": seg.T,
        "w_sh_gate": w_sh_gate.astype(BF16), "w_sh_up": w_sh_up.astype(BF16),
        "w_sh_down": w_sh_down.astype(BF16),
        "w_exp_gate": w_exp_gate.astype(BF16), "w_exp_up": w_exp_up.astype(BF16),
        "w_exp_down": w_exp_down.astype(BF16),
    }

    mod = _mod_call(jnp.concatenate([c_prompt, c_sample], axis=0), w_ada, b_ada)
    zeros = lambda shape: jnp.zeros(shape, x_prompt.dtype)
    yp, wkv_p, shift_p, conv_p = _layer(
        x_prompt, mod[:bp], zeros((bp, H, N, N)), zeros((bp, state_shift.shape[1])),
        zeros((bp, CONV_W - 1, D)), pw, per_token=False,
        tiles={"proj": 1024, "prep": 256, "wkv": 64, "post": 256, "moe": 1024})
    ts = x_sample.shape[1]
    ys, wkv_s, shift_s, conv_s = _layer(
        x_sample, mod[bp:], state_wkv, state_shift, state_conv, pw, per_token=True,
        tiles={"proj": bs * ts, "prep": ts, "wkv": ts, "post": 256, "moe": bs * ts})
    return (yp, ys, wkv_p, shift_p, conv_p, wkv_s, shift_s, conv_s)
```

```python
import functools

import jax
import jax.numpy as jnp
from jax import lax
from jax.experimental import pallas as pl
from jax.experimental.pallas import tpu as pltpu

F32 = jnp.float32
BF16 = jnp.bfloat16

D = 1024
H = 16
N = 64
E = 64
DE = 256
TOP_K = 6
ROUTED_SCALE = 2.5
CONV_W = 31
RMS_EPS = 1e-6
LN_EPS = 1e-5
GN_EPS = N * 1e-5

C_R, C_K, C_V = 0, 1024, 2048
C_XW, C_XA, C_XG = 3072, 3200, 3328
NRW = 3584
C_CVA, C_CVB, C_GA, C_GB = 3584, 4608, 5632, 6656
NIN = 7680

LANES = 128
HG = 4
GL = HG * N
NG = H // HG
WKV_SEQS = 8
WKV_UNROLL = 4
VMEM_LIMIT = 56 * 1024 * 1024


def _dot(a, b):
    return jnp.dot(a, b, preferred_element_type=F32)


def _sigmoid(x):
    return jax.nn.sigmoid(x)


def _split_bf16(x):
    hi = x.astype(BF16)
    lo = (x - hi.astype(F32)).astype(BF16)
    return hi, lo


def _head_sum(x, seg, exp):
    hi, lo = _split_bf16(x)
    s = _dot(hi, seg) + _dot(lo, seg)
    shi, slo = _split_bf16(s)
    return _dot(shi, exp) + _dot(slo, exp)


def _params(sem, vmem=VMEM_LIMIT):
    return pltpu.CompilerParams(dimension_semantics=sem, vmem_limit_bytes=vmem)


def _mod_body(c_ref, w_ref, b_ref, o_ref):
    c = c_ref[...]
    s = (c * _sigmoid(c)).astype(BF16)
    o_ref[...] = _dot(s, w_ref[...].astype(BF16)) + b_ref[...]


def _mod_call(c_all, w_ada, b_ada):
    rows = c_all.shape[0]
    tn = 512
    return pl.pallas_call(
        _mod_body,
        grid=(6 * D // tn,),
        in_specs=[pl.BlockSpec((rows, D), lambda j: (0, 0)),
                  pl.BlockSpec((D, tn), lambda j: (0, j)),
                  pl.BlockSpec((1, tn), lambda j: (0, j))],
        out_specs=pl.BlockSpec((rows, tn), lambda j: (0, j)),
        out_shape=jax.ShapeDtypeStruct((rows, 6 * D), F32),
        compiler_params=_params(("arbitrary",)),
        name="ada_mod",
    )(c_all, w_ada, b_ada.reshape(1, 6 * D))


def _mod_spec(per_token, tm, rows_per_seq):
    if per_token:
        return pl.BlockSpec((None, tm, D), lambda i, *_: (0, i, 0))
    return pl.BlockSpec((None, 1, D), lambda i, *_: (i * tm // rows_per_seq, 0, 0))


def _proj_body(x_ref, sc_ref, sh_ref, g_ref, w_ref, o_ref, h_scr):
    @pl.when(pl.program_id(1) == 0)
    def _():
        x = x_ref[...]
        y = x * lax.rsqrt(jnp.mean(x * x, axis=-1, keepdims=True) + RMS_EPS) * g_ref[...]
        h_scr[...] = (y * (1.0 + sc_ref[...]) + sh_ref[...]).astype(BF16)

    o_ref[...] = _dot(h_scr[...], w_ref[...])


def _proj_call(x, sc, sh, g, w_in_p, per_token, tm, rows_per_seq):
    ntok = x.shape[0]
    tn = 1280
    mspec = _mod_spec(per_token, tm, rows_per_seq)
    return pl.pallas_call(
        _proj_body,
        grid=(ntok // tm, NIN // tn),
        in_specs=[pl.BlockSpec((tm, D), lambda i, j: (i, 0)),
                  mspec, mspec,
                  pl.BlockSpec((1, D), lambda i, j: (0, 0)),
                  pl.BlockSpec((D, tn), lambda i, j: (0, j))],
        out_specs=pl.BlockSpec((tm, tn), lambda i, j: (i, j)),
        out_shape=jax.ShapeDtypeStruct((ntok, NIN), F32),
        scratch_shapes=[pltpu.VMEM((tm, D), BF16)],
        compiler_params=_params(("arbitrary", "arbitrary")),
        name="in_proj",
    )(x, sc, sh, g, w_in_p)


def _prep_body(proj_ref, sh0_ref, cv0_ref, mu_ref, w0_ref, a0_ref, kk_ref, ka_ref, rk_ref,
               wd2_ref, wa2_ref, wg2_ref, dww_ref, dwb_ref, lng_ref, lnb_ref, wco_ref,
               seg_ref, exp_ref,
               al_o, be_o, k_o, r_o, w_o, v_o, g_o, bo_o, sga_o, yb_o, cv_o,
               buf_scr, ext_scr, part_scr, *, tm):
    ti = pl.program_id(1)

    @pl.when(ti == 0)
    def _():
        buf_scr[7:8, :] = sh0_ref[...]
        ext_scr[0:2, :] = jnp.zeros((2, D), F32)
        ext_scr[2:32, :] = cv0_ref[...]

    p = proj_ref[:, 0:NRW]
    buf_scr[8:8 + tm, :] = p
    prev = buf_scr[7:7 + tm, :]
    xs = p + (prev - p) * mu_ref[...]
    buf_scr[7:8, :] = p[tm - 1:tm, :]

    r = xs[:, C_R:C_R + D]
    k = xs[:, C_K:C_K + D]
    v = xs[:, C_V:C_V + D]
    xw = xs[:, C_XW:C_XW + 128]
    xa = xs[:, C_XA:C_XA + 128]
    xg = xs[:, C_XG:C_XG + 256]

    lw = _dot(jnp.tanh(xw).astype(BF16), wd2_ref[...])
    z = -(w0_ref[...] + lw)
    softplus = jnp.maximum(z, 0.0) + jnp.log(1.0 + jnp.exp(-jnp.abs(z)))
    decay = jnp.exp(-jnp.exp(-softplus - 0.5))
    a = _sigmoid(a0_ref[...] + _dot(xa.astype(BF16), wa2_ref[...]))
    g_o[...] = _dot(_sigmoid(xg).astype(BF16), wg2_ref[...])

    seg = seg_ref[...]
    exp = exp_ref[...]
    kk = k * kk_ref[...]
    kk = kk / jnp.maximum(jnp.sqrt(_head_sum(kk * kk, seg, exp)), 1e-12)
    k2 = k * (1.0 + (a - 1.0) * ka_ref[...])
    bo_o[...] = _head_sum(r * k2 * rk_ref[...], seg, exp) * v
    al_o[...] = -kk
    be_o[...] = kk * a
    k_o[...] = k2
    r_o[...] = r
    w_o[...] = decay
    v_o[...] = v

    glu = proj_ref[:, C_CVA:C_CVA + D] * _sigmoid(proj_ref[:, C_CVB:C_CVB + D])
    ext_scr[32:32 + tm, :] = glu
    acc = jnp.zeros((tm, D), F32) + dwb_ref[...]
    for s in range(8):
        rows = tm if s == 0 else tm + 8
        part = None
        for q in range(5):
            o = 8 * q + s
            if 2 <= o <= CONV_W + 1:
                term = ext_scr[8 * q:8 * q + rows, :] * dww_ref[o - 2:o - 1, :]
                part = term if part is None else part + term
        if s == 0:
            acc = acc + part
        else:
            part_scr[...] = part
            acc = acc + part_scr[s:s + tm, :]
    tail = ext_scr[tm:tm + 32, :]
    cv_o[...] = tail[2:32, :]
    ext_scr[0:32, :] = tail
    mean = jnp.mean(acc, axis=-1, keepdims=True)
    cen = acc - mean
    var = jnp.mean(cen * cen, axis=-1, keepdims=True)
    ln = cen * lax.rsqrt(var + LN_EPS) * lng_ref[...] + lnb_ref[...]
    u = (ln * _sigmoid(ln)).astype(BF16)
    yb = _dot(u, wco_ref[...])
    sga_o[...] = _sigmoid(proj_ref[:, C_GA:C_GA + D])
    yb_o[...] = _sigmoid(proj_ref[:, C_GB:C_GB + D]) * yb


def _prep_call(proj, shift0_p, conv0, pw, bn, t, tm):
    nt = t // tm
    row = lambda shape: pl.BlockSpec(shape, lambda b, i: (0,) * len(shape))
    tok = pl.BlockSpec((None, tm, D), lambda b, i: (b, i, 0))
    ins = [pl.BlockSpec((None, tm, NIN), lambda b, i: (b, i, 0)),
           pl.BlockSpec((None, 1, NRW), lambda b, i: (b, 0, 0)),
           pl.BlockSpec((None, CONV_W - 1, D), lambda b, i: (b, 0, 0)),
           row((1, NRW)), row((1, D)), row((1, D)), row((1, D)), row((1, D)), row((1, D)),
           row((128, D)), row((128, D)), row((256, D)),
           row((CONV_W, D)), row((1, D)), row((1, D)), row((1, D)), row((D, D)),
           row((D, LANES)), row((LANES, D))]
    outs = [tok] * 10 + [pl.BlockSpec((None, CONV_W - 1, D), lambda b, i: (b, 0, 0))]
    out_shape = [jax.ShapeDtypeStruct((bn, t, D), F32)] * 10 + [
        jax.ShapeDtypeStruct((bn, CONV_W - 1, D), F32)]
    return pl.pallas_call(
        functools.partial(_prep_body, tm=tm),
        grid=(bn, nt),
        in_specs=ins,
        out_specs=outs,
        out_shape=out_shape,
        scratch_shapes=[pltpu.VMEM((tm + 8, NRW), F32), pltpu.VMEM((tm + 32, D), F32),
                        pltpu.VMEM((tm + 8, D), F32)],
        compiler_params=_params(("arbitrary", "arbitrary")),
        name="rwkv_prep_conv",
    )(proj, shift0_p, conv0, pw["mu"], pw["w0"], pw["a0"], pw["k_k"], pw["k_a"], pw["r_k"],
      pw["wd2"], pw["wa2"], pw["wg2"], pw["dw_w"], pw["dw_b"], pw["ln_g"], pw["ln_b"],
      pw["w_conv_out"], pw["seg"], pw["exp"])


def _wkv_body(al_ref, be_ref, k_ref, r_ref, w_ref, vh_ref, *rest, tb, zero_init):
    s0_ref = None if zero_init else rest[0]
    y_ref, so_ref, s_scr, mv_scr = rest[-4:]
    ti = pl.program_id(1)

    @pl.when(ti == 0)
    def _():
        if zero_init:
            s_scr[...] = jnp.zeros(s_scr.shape, F32)
        else:
            for b in range(WKV_SEQS):
                for g in range(NG):
                    s_scr[b, g] = jnp.concatenate(
                        [s0_ref[b, HG * g + h] for h in range(HG)], axis=-1)

    row_x = lax.broadcasted_iota(jnp.int32, (8, GL), 0)
    lane_x = lax.broadcasted_iota(jnp.int32, (8, GL), 1)
    head_mask = (lane_x // N == row_x % HG).astype(F32)
    low_x = row_x < HG
    low_o = lax.broadcasted_iota(jnp.int32, (8, N), 0) < HG

    def matvec(b, g, t_a, t_r):
        lanes = pl.ds(g * GL, GL)
        a_b = jnp.broadcast_to(al_ref[b, pl.ds(t_a, 1), lanes], (8, GL))
        r_b = jnp.broadcast_to(r_ref[b, pl.ds(t_r, 1), lanes], (8, GL))
        x = (jnp.where(low_x, r_b, a_b) if g % 2 == 0 else jnp.where(low_x, a_b, r_b)) * head_mask
        s = s_scr[b, g]
        return lax.dot_general(x.astype(BF16), s.astype(BF16), (((1,), (1,)), ((), ())),
                               preferred_element_type=F32)

    groups = [(b, g) for b in range(WKV_SEQS) for g in range(NG)]
    n_grp = len(groups)
    lag = n_grp // 2

    def matvec_y(i, t_a, t_r):
        b, g = groups[i]
        out = matvec(b, g, t_a, t_r)
        mv_scr[b, g] = out
        if g % 2 == 1:
            j = g // 2
            y_ref[b, t_r, 8 * j:8 * j + 8, :] = jnp.where(low_o, mv_scr[b, g - 1], out)

    def update(i, t):
        b, g = groups[i]
        lanes = pl.ds(g * GL, GL)
        j = g // 2
        vt = vh_ref[b, t, 8 * j:8 * j + 8, :]
        out = mv_scr[b, g]
        lt = jnp.where(low_o, vt, out) if g % 2 == 0 else jnp.where(low_o, out, vt)
        k_b = jnp.broadcast_to(k_ref[b, pl.ds(t, 1), lanes], (8, GL))
        be_b = jnp.broadcast_to(be_ref[b, pl.ds(t, 1), lanes], (8, GL))
        q = (jnp.where(low_x, k_b, be_b) if g % 2 == 0
             else jnp.where(low_x, be_b, k_b)) * head_mask
        ds = lax.dot_general(lt.astype(BF16), q.astype(BF16), (((0,), (0,)), ((), ())),
                             preferred_element_type=F32)
        s_scr[b, g] = s_scr[b, g] * w_ref[b, pl.ds(t, 1), lanes] + ds

    for i in range(n_grp):
        matvec_y(i, 0, 0)
        if i >= lag:
            update(i - lag, 0)

    def step(t, carry):
        for i in range(n_grp):
            matvec_y(i, t, t - 1)
            if i < lag:
                update(i + lag, t - 1)
            else:
                update(i - lag, t)
        return carry

    lax.fori_loop(1, tb, step, 0, unroll=WKV_UNROLL)

    for i in range(lag, n_grp):
        update(i, tb - 1)
    for i in range(n_grp):
        matvec_y(i, tb - 1, tb - 1)

    @pl.when(ti == pl.num_programs(1) - 1)
    def _():
        for b in range(WKV_SEQS):
            for g in range(NG):
                s = s_scr[b, g]
                for h in range(HG):
                    so_ref[b, HG * g + h] = s[:, N * h:N * (h + 1)]


def _wkv_call(al, be, k2, r, w, vh, s0, bn, t, tb):
    seq = pl.BlockSpec((WKV_SEQS, tb, D), lambda i, j: (i, j, 0))
    hv = pl.BlockSpec((WKV_SEQS, tb, H, N), lambda i, j: (i, j, 0, 0))
    st = pl.BlockSpec((WKV_SEQS, H, N, N), lambda i, j: (i, 0, 0, 0))
    zero_init = s0 is None
    return pl.pallas_call(
        functools.partial(_wkv_body, tb=tb, zero_init=zero_init),
        grid=(bn // WKV_SEQS, t // tb),
        in_specs=[seq, seq, seq, seq, seq, hv] + ([] if zero_init else [st]),
        out_specs=[hv, st],
        out_shape=[jax.ShapeDtypeStruct((bn, t, H, N), F32),
                   jax.ShapeDtypeStruct((bn, H, N, N), F32)],
        scratch_shapes=[pltpu.VMEM((WKV_SEQS, NG, N, GL), F32),
                        pltpu.VMEM((WKV_SEQS, NG, 8, N), F32)],
        compiler_params=_params(("arbitrary", "arbitrary")),
        name="wkv7_scan",
    )(*((al, be, k2, r, w, vh) + (() if zero_init else (s0,))))


def _post_body(y_ref, bo_ref, g_ref, sga_ref, yb_ref, x_ref, gtm_ref, scf_ref, shf_ref,
               gng_ref, gnb_ref, wo_ref, wout_ref, postg_ref, preg_ref,
               wrh_ref, wrl_ref, rb_ref, seg_ref, exp_ref,
               x1_o, hf_o, gates_o):
    seg = seg_ref[...]
    exp = exp_ref[...]
    y = y_ref[...]
    mu = _head_sum(y, seg, exp) * (1.0 / N)
    cen = y - mu
    var = _head_sum(cen * cen, seg, exp) * (1.0 / N)
    yn = cen * lax.rsqrt(var + GN_EPS) * gng_ref[...] + gnb_ref[...] + bo_ref[...]
    ya = _dot((yn * g_ref[...]).astype(BF16), wo_ref[...])
    merged = sga_ref[...] * ya + yb_ref[...]
    z = _dot(merged.astype(BF16), wout_ref[...])
    zn = z * lax.rsqrt(jnp.mean(z * z, axis=-1, keepdims=True) + RMS_EPS) * postg_ref[...]
    x1 = x_ref[...] + gtm_ref[...] * zn
    x1_o[...] = x1
    hn = x1 * lax.rsqrt(jnp.mean(x1 * x1, axis=-1, keepdims=True) + RMS_EPS) * preg_ref[...]
    hf = hn * (1.0 + scf_ref[...]) + shf_ref[...]
    hf_o[...] = hf.astype(BF16)

    hh, hl = _split_bf16(hf)
    logits = _dot(hh, wrh_ref[...]) + _dot(hl, wrh_ref[...]) + _dot(hh, wrl_ref[...])
    scores = _sigmoid(logits)
    lane = lax.broadcasted_iota(jnp.int32, scores.shape, 1).astype(F32)
    neg = jnp.float32(-jnp.inf)
    sb = jnp.where(lane < E, scores + rb_ref[...], neg)
    picked = jnp.zeros(scores.shape, F32)
    for _ in range(TOP_K):
        m = jnp.max(sb, axis=-1, keepdims=True)
        first = jnp.min(jnp.where(sb == m, lane, float(LANES)), axis=-1, keepdims=True)
        pick = lane == first
        picked = jnp.where(pick, 1.0, picked)
        sb = jnp.where(pick, neg, sb)
    sel = scores * picked
    gates_o[...] = sel / jnp.sum(sel, axis=-1, keepdims=True) * ROUTED_SCALE


def _post_call(y, bo, g, sga, yb, x, gtm, scf, shf, pw, per_token, tm, rows_per_seq):
    ntok = x.shape[0]
    tok = pl.BlockSpec((tm, D), lambda i: (i, 0))
    row = lambda shape: pl.BlockSpec(shape, lambda i: (0,) * len(shape))
    mspec = _mod_spec(per_token, tm, rows_per_seq)
    return pl.pallas_call(
        _post_body,
        grid=(ntok // tm,),
        in_specs=[tok] * 6 + [mspec] * 3 + [
            row((1, D)), row((1, D)), row((D, D)), row((D, D)), row((1, D)), row((1, D)),
            row((D, LANES)), row((D, LANES)), row((1, LANES)), row((D, LANES)), row((LANES, D))],
        out_specs=[tok, tok, pl.BlockSpec((tm, LANES), lambda i: (i, 0))],
        out_shape=[jax.ShapeDtypeStruct((ntok, D), F32),
                   jax.ShapeDtypeStruct((ntok, D), BF16),
                   jax.ShapeDtypeStruct((ntok, LANES), F32)],
        compiler_params=_params(("arbitrary",)),
        name="mix_post_router",
    )(y, bo, g, sga, yb, x, gtm, scf, shf, pw["gn_g"], pw["gn_b"], pw["w_o_rwkv"], pw["w_out"],
      pw["mix_post_g"], pw["ffn_pre_g"], pw["wr_hi"], pw["wr_lo"], pw["router_bias"],
      pw["seg"], pw["exp"])


MOE_EB = 4


def _moe_body(hf_ref, gates_ref, x1_ref, gtf_ref, postg_ref, wsg_ref, wsu_ref, wsd_ref,
              wg_ref, wu_ref, wd_ref, o_ref, acc_scr):
    e0 = pl.program_id(1)
    t = hf_ref[...]

    @pl.when(e0 == 0)
    def _():
        hg = _dot(t, wsg_ref[...])
        he = hg * _sigmoid(hg) * _dot(t, wsu_ref[...])
        acc_scr[...] = _dot(he.astype(BF16), wsd_ref[...])

    gates = gates_ref[...]
    lane = lax.broadcasted_iota(jnp.int32, gates.shape, 1)
    acc = acc_scr[...]
    for i in range(MOE_EB):
        gcol = jnp.sum(jnp.where(lane == e0 * MOE_EB + i, gates, 0.0), axis=-1, keepdims=True)
        hg = _dot(t, wg_ref[i])
        he = hg * _sigmoid(hg) * _dot(t, wu_ref[i])
        acc = acc + gcol * _dot(he.astype(BF16), wd_ref[i])
    acc_scr[...] = acc

    @pl.when(e0 == pl.num_programs(1) - 1)
    def _():
        out = acc_scr[...]
        on = out * lax.rsqrt(jnp.mean(out * out, axis=-1, keepdims=True) + RMS_EPS) * postg_ref[...]
        o_ref[...] = x1_ref[...] + gtf_ref[...] * on


def _moe_call(hf, gates, x1, gtf, pw, per_token, tm, rows_per_seq):
    ntok = hf.shape[0]
    tok = pl.BlockSpec((tm, D), lambda i, e: (i, 0))
    row = lambda shape: pl.BlockSpec(shape, lambda i, e: (0,) * len(shape))
    mspec = _mod_spec(per_token, tm, rows_per_seq)
    return pl.pallas_call(
        _moe_body,
        grid=(ntok // tm, E // MOE_EB),
        in_specs=[tok, pl.BlockSpec((tm, LANES), lambda i, e: (i, 0)), tok, mspec,
                  row((1, D)), row((D, DE)), row((D, DE)), row((DE, D)),
                  pl.BlockSpec((MOE_EB, D, DE), lambda i, e: (e, 0, 0)),
                  pl.BlockSpec((MOE_EB, D, DE), lambda i, e: (e, 0, 0)),
                  pl.BlockSpec((MOE_EB, DE, D), lambda i, e: (e, 0, 0))],
        out_specs=tok,
        out_shape=jax.ShapeDtypeStruct((ntok, D), F32),
        scratch_shapes=[pltpu.VMEM((tm, D), F32)],
        compiler_params=_params(("arbitrary", "arbitrary")),
        name="moe_ffn",
    )(hf, gates, x1, gtf, pw["ffn_post_g"], pw["w_sh_gate"], pw["w_sh_up"], pw["w_sh_down"],
      pw["w_exp_gate"], pw["w_exp_up"], pw["w_exp_down"])


def _pad_rwkv_cols(a):
    pad = lambda z, n: jnp.pad(z, [(0, 0)] * (z.ndim - 1) + [(0, n - z.shape[-1])])
    return jnp.concatenate([a[..., :3072], pad(a[..., 3072:3136], 128), pad(a[..., 3136:3200], 128),
                            pad(a[..., 3200:3360], 256)], axis=-1)


def _unpad_rwkv_cols(a):
    return jnp.concatenate([a[..., :3072], a[..., C_XW:C_XW + 64], a[..., C_XA:C_XA + 64],
                            a[..., C_XG:C_XG + 160]], axis=-1)


def _layer(x, mod, wkv0, shift0, conv0, pw, per_token, tiles):
    bn, t, _ = x.shape
    ntok = bn * t
    xf = x.reshape(ntok, D)
    chunks = [mod[:, i * D:(i + 1) * D] for i in range(6)]
    if per_token:
        chunks = [jnp.repeat(c, t, axis=0).reshape(1, ntok, D) for c in chunks]
    else:
        chunks = [c.reshape(bn, 1, D) for c in chunks]
    sh_m, sc_m, gt_m, sh_f, sc_f, gt_f = chunks

    proj = _proj_call(xf, sc_m, sh_m, pw["mix_pre_g"], pw["w_in"], per_token, tiles["proj"], t)
    proj = proj.reshape(bn, t, NIN)
    shift_out = _unpad_rwkv_cols(proj[:, t - 1, :NRW])

    shift0_p = _pad_rwkv_cols(shift0).reshape(bn, 1, NRW)
    al, be, k2, r, w, v, g, bo, sga, yb, conv_out = _prep_call(
        proj, shift0_p, conv0, pw, bn, t, tiles["prep"])

    flat = lambda a: a.reshape(ntok, D)
    vh = v.reshape(bn, t, H, N)
    yh, wkv_out = _wkv_call(al, be, k2, r, w, vh, wkv0, bn, t, tiles["wkv"])
    y = yh.reshape(ntok, D)

    x1, hf, gates = _post_call(y, flat(bo), flat(g), flat(sga), flat(yb), xf, gt_m, sc_f, sh_f,
                               pw, per_token, tiles["post"], t)
    x2 = _moe_call(hf, gates, x1, gt_f, pw, per_token, tiles["moe"], t)
    return x2.reshape(bn, t, D), wkv_out, shift_out, conv_out


def kernel(x_prompt, x_sample, state_wkv, state_shift, state_conv, c_prompt, c_sample, w_ada, b_ada, mix_pre_g, mix_post_g, w_in, mu_shift, w0, w_decay2, a0, w_aaa2, w_gate2, k_k, k_a, r_k, gn_g, gn_b, w_o_rwkv, dw_w, dw_b, conv_ln_g, conv_ln_b, w_conv_out, w_out, ffn_pre_g, ffn_post_g, w_router, router_bias, w_exp_gate, w_exp_up, w_exp_down, w_sh_gate, w_sh_up, w_sh_down):
    bp = x_prompt.shape[0]
    bs = x_sample.shape[0]
    row = lambda a: a.reshape(1, -1).astype(F32)
    padr = lambda a, n: jnp.pad(a, ((0, n - a.shape[0]), (0, 0)))
    padc = lambda a, n: jnp.pad(a, ((0, 0), (0, n - a.shape[1])))

    w_in_p = jnp.concatenate(
        [w_in[:, :3072], padc(w_in[:, 3072:3136], 128), padc(w_in[:, 3136:3200], 128),
         padc(w_in[:, 3200:3360], 256), w_in[:, 3360:]], axis=1).astype(BF16)
    head_of_col = jnp.arange(D, dtype=jnp.int32) // N
    seg = (head_of_col[:, None] == jnp.arange(LANES, dtype=jnp.int32)[None, :]).astype(BF16)
    wr = padc(w_router, LANES)
    wr_hi = wr.astype(BF16)
    pw = {
        "w_in": w_in_p,
        "mix_pre_g": row(mix_pre_g), "mix_post_g": row(mix_post_g),
        "ffn_pre_g": row(ffn_pre_g), "ffn_post_g": row(ffn_post_g),
        "mu": row(_pad_rwkv_cols(mu_shift)), "w0": row(w0), "a0": row(a0),
        "k_k": row(k_k), "k_a": row(k_a), "r_k": row(r_k),
        "wd2": padr(w_decay2, 128).astype(BF16), "wa2": padr(w_aaa2, 128).astype(BF16),
        "wg2": padr(w_gate2, 256).astype(BF16),
        "dw_w": dw_w, "dw_b": row(dw_b), "ln_g": row(conv_ln_g), "ln_b": row(conv_ln_b),
        "w_conv_out": w_conv_out.astype(BF16),
        "gn_g": row(gn_g), "gn_b": row(gn_b),
        "w_o_rwkv": w_o_rwkv.astype(BF16), "w_out": w_out.astype(BF16),
        "wr_hi": wr_hi, "wr_lo": (wr - wr_hi.astype(F32)).astype(BF16),
        "router_bias": padc(row(router_bias), LANES),
        "seg": seg, "exp": seg.T,
        "w_sh_gate": w_sh_gate.astype(BF16), "w_sh_up": w_sh_up.astype(BF16),
        "w_sh_down": w_sh_down.astype(BF16),
        "w_exp_gate": w_exp_gate.astype(BF16), "w_exp_up": w_exp_up.astype(BF16),
        "w_exp_down": w_exp_down.astype(BF16),
    }

    mod = _mod_call(jnp.concatenate([c_prompt, c_sample], axis=0), w_ada, b_ada)
    zeros = lambda shape: jnp.zeros(shape, x_prompt.dtype)
    yp, wkv_p, shift_p, conv_p = _layer(
        x_prompt, mod[:bp], None, zeros((bp, state_shift.shape[1])),
        zeros((bp, CONV_W - 1, D)), pw, per_token=False,
        tiles={"proj": 1024, "prep": 256, "wkv": 64, "post": 256, "moe": 1024})
    ts = x_sample.shape[1]
    ys, wkv_s, shift_s, conv_s = _layer(
        x_sample, mod[bp:], state_wkv, state_shift, state_conv, pw, per_token=True,
        tiles={"proj": bs * ts, "prep": ts, "wkv": ts, "post": 256, "moe": bs * ts})
    return (yp, ys, wkv_p, shift_p, conv_p, wkv_s, shift_s, conv_s)
```

```python
import functools

import jax
import jax.numpy as jnp
from jax import lax
from jax.experimental import pallas as pl
from jax.experimental.pallas import tpu as pltpu

F32 = jnp.float32
BF16 = jnp.bfloat16

D = 1024
H = 16
N = 64
E = 64
DE = 256
TOP_K = 6
ROUTED_SCALE = 2.5
CONV_W = 31
RMS_EPS = 1e-6
LN_EPS = 1e-5
GN_EPS = N * 1e-5

C_R, C_K, C_V = 0, 1024, 2048
C_XW, C_XA, C_XG = 3072, 3200, 3328
NRW = 3584
C_CVA, C_CVB, C_GA, C_GB = 3584, 4608, 5632, 6656
NIN = 7680

LANES = 128
HG = 4
GL = HG * N
NG = H // HG
WKV_SEQS = 8
WKV_UNROLL = 4
VMEM_LIMIT = 56 * 1024 * 1024


def _dot(a, b):
    return jnp.dot(a, b, preferred_element_type=F32)


def _sigmoid(x):
    return jax.nn.sigmoid(x)


def _split_bf16(x):
    hi = x.astype(BF16)
    lo = (x - hi.astype(F32)).astype(BF16)
    return hi, lo


def _head_sum(x, seg, exp):
    hi, lo = _split_bf16(x)
    s = _dot(hi, seg) + _dot(lo, seg)
    shi, slo = _split_bf16(s)
    return _dot(shi, exp) + _dot(slo, exp)


def _params(sem, vmem=VMEM_LIMIT):
    return pltpu.CompilerParams(dimension_semantics=sem, vmem_limit_bytes=vmem)


def _mod_body(c_ref, w_ref, b_ref, o_ref):
    c = c_ref[...]
    s = (c * _sigmoid(c)).astype(BF16)
    o_ref[...] = _dot(s, w_ref[...].astype(BF16)) + b_ref[...]


def _mod_call(c_all, w_ada, b_ada):
    rows = c_all.shape[0]
    tn = 512
    return pl.pallas_call(
        _mod_body,
        grid=(6 * D // tn,),
        in_specs=[pl.BlockSpec((rows, D), lambda j: (0, 0)),
                  pl.BlockSpec((D, tn), lambda j: (0, j)),
                  pl.BlockSpec((1, tn), lambda j: (0, j))],
        out_specs=pl.BlockSpec((rows, tn), lambda j: (0, j)),
        out_shape=jax.ShapeDtypeStruct((rows, 6 * D), F32),
        compiler_params=_params(("arbitrary",)),
        name="ada_mod",
    )(c_all, w_ada, b_ada.reshape(1, 6 * D))


def _mod_spec(per_token, tm, rows_per_seq):
    if per_token:
        return pl.BlockSpec((None, tm, D), lambda i, *_: (0, i, 0))
    return pl.BlockSpec((None, 1, D), lambda i, *_: (i * tm // rows_per_seq, 0, 0))


def _proj_body(x_ref, sc_ref, sh_ref, g_ref, w_ref, o_ref, h_scr):
    @pl.when(pl.program_id(1) == 0)
    def _():
        x = x_ref[...]
        y = x * lax.rsqrt(jnp.mean(x * x, axis=-1, keepdims=True) + RMS_EPS) * g_ref[...]
        h_scr[...] = (y * (1.0 + sc_ref[...]) + sh_ref[...]).astype(BF16)

    o_ref[...] = _dot(h_scr[...], w_ref[...])


def _proj_call(x, sc, sh, g, w_in_p, per_token, tm, rows_per_seq):
    ntok = x.shape[0]
    tn = 1280
    mspec = _mod_spec(per_token, tm, rows_per_seq)
    return pl.pallas_call(
        _proj_body,
        grid=(ntok // tm, NIN // tn),
        in_specs=[pl.BlockSpec((tm, D), lambda i, j: (i, 0)),
                  mspec, mspec,
                  pl.BlockSpec((1, D), lambda i, j: (0, 0)),
                  pl.BlockSpec((D, tn), lambda i, j: (0, j))],
        out_specs=pl.BlockSpec((tm, tn), lambda i, j: (i, j)),
        out_shape=jax.ShapeDtypeStruct((ntok, NIN), F32),
        scratch_shapes=[pltpu.VMEM((tm, D), BF16)],
        compiler_params=_params(("arbitrary", "arbitrary")),
        name="in_proj",
    )(x, sc, sh, g, w_in_p)


def _prep_body(proj_ref, sh0_ref, cv0_ref, mu_ref, w0_ref, a0_ref, kk_ref, ka_ref, rk_ref,
               wd2_ref, wa2_ref, wg2_ref, dww_ref, dwb_ref, lng_ref, lnb_ref, wco_ref,
               seg_ref, exp_ref,
               al_o, be_o, k_o, r_o, w_o, v_o, g_o, bo_o, sga_o, yb_o, cv_o,
               buf_scr, ext_scr, part_scr, *, tm):
    ti = pl.program_id(1)

    @pl.when(ti == 0)
    def _():
        buf_scr[7:8, :] = sh0_ref[...]
        ext_scr[0:2, :] = jnp.zeros((2, D), F32)
        ext_scr[2:32, :] = cv0_ref[...]

    p = proj_ref[:, 0:NRW]
    buf_scr[8:8 + tm, :] = p
    prev = buf_scr[7:7 + tm, :]
    xs = p + (prev - p) * mu_ref[...]
    buf_scr[7:8, :] = p[tm - 1:tm, :]

    r = xs[:, C_R:C_R + D]
    k = xs[:, C_K:C_K + D]
    v = xs[:, C_V:C_V + D]
    xw = xs[:, C_XW:C_XW + 128]
    xa = xs[:, C_XA:C_XA + 128]
    xg = xs[:, C_XG:C_XG + 256]

    lw = _dot(jnp.tanh(xw).astype(BF16), wd2_ref[...])
    z = -(w0_ref[...] + lw)
    softplus = jnp.maximum(z, 0.0) + jnp.log(1.0 + jnp.exp(-jnp.abs(z)))
    decay = jnp.exp(-jnp.exp(-softplus - 0.5))
    a = _sigmoid(a0_ref[...] + _dot(xa.astype(BF16), wa2_ref[...]))
    g_o[...] = _dot(_sigmoid(xg).astype(BF16), wg2_ref[...])

    seg = seg_ref[...]
    exp = exp_ref[...]
    kk = k * kk_ref[...]
    kk = kk / jnp.maximum(jnp.sqrt(_head_sum(kk * kk, seg, exp)), 1e-12)
    k2 = k * (1.0 + (a - 1.0) * ka_ref[...])
    bo_o[...] = _head_sum(r * k2 * rk_ref[...], seg, exp) * v
    al_o[...] = -kk
    be_o[...] = kk * a
    k_o[...] = k2
    r_o[...] = r
    w_o[...] = decay
    v_o[...] = v.astype(BF16)

    glu = proj_ref[:, C_CVA:C_CVA + D] * _sigmoid(proj_ref[:, C_CVB:C_CVB + D])
    ext_scr[32:32 + tm, :] = glu
    acc = jnp.zeros((tm, D), F32) + dwb_ref[...]
    for s in range(8):
        rows = tm if s == 0 else tm + 8
        part = None
        for q in range(5):
            o = 8 * q + s
            if 2 <= o <= CONV_W + 1:
                term = ext_scr[8 * q:8 * q + rows, :] * dww_ref[o - 2:o - 1, :]
                part = term if part is None else part + term
        if s == 0:
            acc = acc + part
        else:
            part_scr[...] = part
            acc = acc + part_scr[s:s + tm, :]
    tail = ext_scr[tm:tm + 32, :]
    cv_o[...] = tail[2:32, :]
    ext_scr[0:32, :] = tail
    mean = jnp.mean(acc, axis=-1, keepdims=True)
    cen = acc - mean
    var = jnp.mean(cen * cen, axis=-1, keepdims=True)
    ln = cen * lax.rsqrt(var + LN_EPS) * lng_ref[...] + lnb_ref[...]
    u = (ln * _sigmoid(ln)).astype(BF16)
    yb = _dot(u, wco_ref[...])
    sga_o[...] = _sigmoid(proj_ref[:, C_GA:C_GA + D])
    yb_o[...] = _sigmoid(proj_ref[:, C_GB:C_GB + D]) * yb


def _prep_call(proj, shift0_p, conv0, pw, bn, t, tm):
    nt = t // tm
    row = lambda shape: pl.BlockSpec(shape, lambda b, i: (0,) * len(shape))
    tok = pl.BlockSpec((None, tm, D), lambda b, i: (b, i, 0))
    ins = [pl.BlockSpec((None, tm, NIN), lambda b, i: (b, i, 0)),
           pl.BlockSpec((None, 1, NRW), lambda b, i: (b, 0, 0)),
           pl.BlockSpec((None, CONV_W - 1, D), lambda b, i: (b, 0, 0)),
           row((1, NRW)), row((1, D)), row((1, D)), row((1, D)), row((1, D)), row((1, D)),
           row((128, D)), row((128, D)), row((256, D)),
           row((CONV_W, D)), row((1, D)), row((1, D)), row((1, D)), row((D, D)),
           row((D, LANES)), row((LANES, D))]
    outs = [tok] * 10 + [pl.BlockSpec((None, CONV_W - 1, D), lambda b, i: (b, 0, 0))]
    out_dtypes = [F32] * 5 + [BF16] + [F32] * 4
    out_shape = [jax.ShapeDtypeStruct((bn, t, D), dt) for dt in out_dtypes] + [
        jax.ShapeDtypeStruct((bn, CONV_W - 1, D), F32)]
    return pl.pallas_call(
        functools.partial(_prep_body, tm=tm),
        grid=(bn, nt),
        in_specs=ins,
        out_specs=outs,
        out_shape=out_shape,
        scratch_shapes=[pltpu.VMEM((tm + 8, NRW), F32), pltpu.VMEM((tm + 32, D), F32),
                        pltpu.VMEM((tm + 8, D), F32)],
        compiler_params=_params(("arbitrary", "arbitrary")),
        name="rwkv_prep_conv",
    )(proj, shift0_p, conv0, pw["mu"], pw["w0"], pw["a0"], pw["k_k"], pw["k_a"], pw["r_k"],
      pw["wd2"], pw["wa2"], pw["wg2"], pw["dw_w"], pw["dw_b"], pw["ln_g"], pw["ln_b"],
      pw["w_conv_out"], pw["seg"], pw["exp"])


def _wkv_body(al_ref, be_ref, k_ref, r_ref, w_ref, vh_ref, *rest, tb, zero_init):
    s0_ref = None if zero_init else rest[0]
    y_ref, so_ref, s_scr, mv_scr = rest[-4:]
    ti = pl.program_id(1)

    @pl.when(ti == 0)
    def _():
        if zero_init:
            s_scr[...] = jnp.zeros(s_scr.shape, F32)
        else:
            for b in range(WKV_SEQS):
                for g in range(NG):
                    s_scr[b, g] = jnp.concatenate(
                        [s0_ref[b, HG * g + h] for h in range(HG)], axis=-1)

    row_x = lax.broadcasted_iota(jnp.int32, (8, GL), 0)
    lane_x = lax.broadcasted_iota(jnp.int32, (8, GL), 1)
    head_mask = (lane_x // N == row_x % HG).astype(F32)
    low_x = row_x < HG
    low_o = lax.broadcasted_iota(jnp.int32, (8, N), 0) < HG

    def matvec(b, g, t_a, t_r):
        lanes = pl.ds(g * GL, GL)
        a_b = jnp.broadcast_to(al_ref[b, pl.ds(t_a, 1), lanes], (8, GL))
        r_b = jnp.broadcast_to(r_ref[b, pl.ds(t_r, 1), lanes], (8, GL))
        x = (jnp.where(low_x, r_b, a_b) if g % 2 == 0 else jnp.where(low_x, a_b, r_b)) * head_mask
        s = s_scr[b, g]
        return lax.dot_general(x.astype(BF16), s.astype(BF16), (((1,), (1,)), ((), ())),
                               preferred_element_type=F32)

    groups = [(b, g) for b in range(WKV_SEQS) for g in range(NG)]
    n_grp = len(groups)
    lag = n_grp // 2

    def matvec_y(i, t_a, t_r):
        b, g = groups[i]
        out = matvec(b, g, t_a, t_r)
        mv_scr[b, g] = out
        if g % 2 == 1:
            j = g // 2
            y_ref[b, t_r, 8 * j:8 * j + 8, :] = jnp.where(low_o, mv_scr[b, g - 1], out)

    def update(i, t):
        b, g = groups[i]
        lanes = pl.ds(g * GL, GL)
        j = g // 2
        vt = vh_ref[b, t, 8 * j:8 * j + 8, :].astype(F32)
        out = mv_scr[b, g]
        lt = jnp.where(low_o, vt, out) if g % 2 == 0 else jnp.where(low_o, out, vt)
        k_b = jnp.broadcast_to(k_ref[b, pl.ds(t, 1), lanes], (8, GL))
        be_b = jnp.broadcast_to(be_ref[b, pl.ds(t, 1), lanes], (8, GL))
        q = (jnp.where(low_x, k_b, be_b) if g % 2 == 0
             else jnp.where(low_x, be_b, k_b)) * head_mask
        ds = lax.dot_general(lt.astype(BF16), q.astype(BF16), (((0,), (0,)), ((), ())),
                             preferred_element_type=F32)
        s_scr[b, g] = s_scr[b, g] * w_ref[b, pl.ds(t, 1), lanes] + ds

    for i in range(n_grp):
        matvec_y(i, 0, 0)
        if i >= lag:
            update(i - lag, 0)

    def step(t, carry):
        for i in range(n_grp):
            matvec_y(i, t, t - 1)
            if i < lag:
                update(i + lag, t - 1)
            else:
                update(i - lag, t)
        return carry

    lax.fori_loop(1, tb, step, 0, unroll=WKV_UNROLL)

    for i in range(lag, n_grp):
        update(i, tb - 1)
    for i in range(n_grp):
        matvec_y(i, tb - 1, tb - 1)

    @pl.when(ti == pl.num_programs(1) - 1)
    def _():
        for b in range(WKV_SEQS):
            for g in range(NG):
                s = s_scr[b, g]
                for h in range(HG):
                    so_ref[b, HG * g + h] = s[:, N * h:N * (h + 1)]


def _wkv_call(al, be, k2, r, w, vh, s0, bn, t, tb):
    seq = pl.BlockSpec((WKV_SEQS, tb, D), lambda i, j: (i, j, 0))
    hv = pl.BlockSpec((WKV_SEQS, tb, H, N), lambda i, j: (i, j, 0, 0))
    st = pl.BlockSpec((WKV_SEQS, H, N, N), lambda i, j: (i, 0, 0, 0))
    zero_init = s0 is None
    return pl.pallas_call(
        functools.partial(_wkv_body, tb=tb, zero_init=zero_init),
        grid=(bn // WKV_SEQS, t // tb),
        in_specs=[seq, seq, seq, seq, seq, hv] + ([] if zero_init else [st]),
        out_specs=[hv, st],
        out_shape=[jax.ShapeDtypeStruct((bn, t, H, N), F32),
                   jax.ShapeDtypeStruct((bn, H, N, N), F32)],
        scratch_shapes=[pltpu.VMEM((WKV_SEQS, NG, N, GL), F32),
                        pltpu.VMEM((WKV_SEQS, NG, 8, N), F32)],
        compiler_params=_params(("arbitrary", "arbitrary")),
        name="wkv7_scan",
    )(*((al, be, k2, r, w, vh) + (() if zero_init else (s0,))))


def _post_body(y_ref, bo_ref, g_ref, sga_ref, yb_ref, x_ref, gtm_ref, scf_ref, shf_ref,
               gng_ref, gnb_ref, wo_ref, wout_ref, postg_ref, preg_ref,
               wrh_ref, wrl_ref, rb_ref, seg_ref, exp_ref,
               x1_o, hf_o, gates_o):
    seg = seg_ref[...]
    exp = exp_ref[...]
    y = y_ref[...]
    mu = _head_sum(y, seg, exp) * (1.0 / N)
    cen = y - mu
    var = _head_sum(cen * cen, seg, exp) * (1.0 / N)
    yn = cen * lax.rsqrt(var + GN_EPS) * gng_ref[...] + gnb_ref[...] + bo_ref[...]
    ya = _dot((yn * g_ref[...]).astype(BF16), wo_ref[...])
    merged = sga_ref[...] * ya + yb_ref[...]
    z = _dot(merged.astype(BF16), wout_ref[...])
    zn = z * lax.rsqrt(jnp.mean(z * z, axis=-1, keepdims=True) + RMS_EPS) * postg_ref[...]
    x1 = x_ref[...] + gtm_ref[...] * zn
    x1_o[...] = x1
    hn = x1 * lax.rsqrt(jnp.mean(x1 * x1, axis=-1, keepdims=True) + RMS_EPS) * preg_ref[...]
    hf = hn * (1.0 + scf_ref[...]) + shf_ref[...]
    hf_o[...] = hf.astype(BF16)

    hh, hl = _split_bf16(hf)
    logits = _dot(hh, wrh_ref[...]) + _dot(hl, wrh_ref[...]) + _dot(hh, wrl_ref[...])
    scores = _sigmoid(logits)
    lane = lax.broadcasted_iota(jnp.int32, scores.shape, 1).astype(F32)
    neg = jnp.float32(-jnp.inf)
    sb = jnp.where(lane < E, scores + rb_ref[...], neg)
    picked = jnp.zeros(scores.shape, F32)
    for _ in range(TOP_K):
        m = jnp.max(sb, axis=-1, keepdims=True)
        first = jnp.min(jnp.where(sb == m, lane, float(LANES)), axis=-1, keepdims=True)
        pick = lane == first
        picked = jnp.where(pick, 1.0, picked)
        sb = jnp.where(pick, neg, sb)
    sel = scores * picked
    gates_o[...] = sel / jnp.sum(sel, axis=-1, keepdims=True) * ROUTED_SCALE


def _post_call(y, bo, g, sga, yb, x, gtm, scf, shf, pw, per_token, tm, rows_per_seq):
    ntok = x.shape[0]
    tok = pl.BlockSpec((tm, D), lambda i: (i, 0))
    row = lambda shape: pl.BlockSpec(shape, lambda i: (0,) * len(shape))
    mspec = _mod_spec(per_token, tm, rows_per_seq)
    return pl.pallas_call(
        _post_body,
        grid=(ntok // tm,),
        in_specs=[tok] * 6 + [mspec] * 3 + [
            row((1, D)), row((1, D)), row((D, D)), row((D, D)), row((1, D)), row((1, D)),
            row((D, LANES)), row((D, LANES)), row((1, LANES)), row((D, LANES)), row((LANES, D))],
        out_specs=[tok, tok, pl.BlockSpec((tm, LANES), lambda i: (i, 0))],
        out_shape=[jax.ShapeDtypeStruct((ntok, D), F32),
                   jax.ShapeDtypeStruct((ntok, D), BF16),
                   jax.ShapeDtypeStruct((ntok, LANES), F32)],
        compiler_params=_params(("arbitrary",)),
        name="mix_post_router",
    )(y, bo, g, sga, yb, x, gtm, scf, shf, pw["gn_g"], pw["gn_b"], pw["w_o_rwkv"], pw["w_out"],
      pw["mix_post_g"], pw["ffn_pre_g"], pw["wr_hi"], pw["wr_lo"], pw["router_bias"],
      pw["seg"], pw["exp"])


MOE_EB = 4


def _moe_body(hf_ref, gates_ref, x1_ref, gtf_ref, postg_ref, wsg_ref, wsu_ref, wsd_ref,
              wg_ref, wu_ref, wd_ref, o_ref, acc_scr):
    e0 = pl.program_id(1)
    t = hf_ref[...]

    @pl.when(e0 == 0)
    def _():
        hg = _dot(t, wsg_ref[...])
        he = hg * _sigmoid(hg) * _dot(t, wsu_ref[...])
        acc_scr[...] = _dot(he.astype(BF16), wsd_ref[...])

    gates = gates_ref[...]
    lane = lax.broadcasted_iota(jnp.int32, gates.shape, 1)
    acc = acc_scr[...]
    for i in range(MOE_EB):
        gcol = jnp.sum(jnp.where(lane == e0 * MOE_EB + i, gates, 0.0), axis=-1, keepdims=True)
        hg = _dot(t, wg_ref[i])
        he = hg * _sigmoid(hg) * _dot(t, wu_ref[i])
        acc = acc + gcol * _dot(he.astype(BF16), wd_ref[i])
    acc_scr[...] = acc

    @pl.when(e0 == pl.num_programs(1) - 1)
    def _():
        out = acc_scr[...]
        on = out * lax.rsqrt(jnp.mean(out * out, axis=-1, keepdims=True) + RMS_EPS) * postg_ref[...]
        o_ref[...] = x1_ref[...] + gtf_ref[...] * on


def _moe_call(hf, gates, x1, gtf, pw, per_token, tm, rows_per_seq):
    ntok = hf.shape[0]
    tok = pl.BlockSpec((tm, D), lambda i, e: (i, 0))
    row = lambda shape: pl.BlockSpec(shape, lambda i, e: (0,) * len(shape))
    mspec = _mod_spec(per_token, tm, rows_per_seq)
    return pl.pallas_call(
        _moe_body,
        grid=(ntok // tm, E // MOE_EB),
        in_specs=[tok, pl.BlockSpec((tm, LANES), lambda i, e: (i, 0)), tok, mspec,
                  row((1, D)), row((D, DE)), row((D, DE)), row((DE, D)),
                  pl.BlockSpec((MOE_EB, D, DE), lambda i, e: (e, 0, 0)),
                  pl.BlockSpec((MOE_EB, D, DE), lambda i, e: (e, 0, 0)),
                  pl.BlockSpec((MOE_EB, DE, D), lambda i, e: (e, 0, 0))],
        out_specs=tok,
        out_shape=jax.ShapeDtypeStruct((ntok, D), F32),
        scratch_shapes=[pltpu.VMEM((tm, D), F32)],
        compiler_params=_params(("arbitrary", "arbitrary")),
        name="moe_ffn",
    )(hf, gates, x1, gtf, pw["ffn_post_g"], pw["w_sh_gate"], pw["w_sh_up"], pw["w_sh_down"],
      pw["w_exp_gate"], pw["w_exp_up"], pw["w_exp_down"])


def _pad_rwkv_cols(a):
    pad = lambda z, n: jnp.pad(z, [(0, 0)] * (z.ndim - 1) + [(0, n - z.shape[-1])])
    return jnp.concatenate([a[..., :3072], pad(a[..., 3072:3136], 128), pad(a[..., 3136:3200], 128),
                            pad(a[..., 3200:3360], 256)], axis=-1)


def _unpad_rwkv_cols(a):
    return jnp.concatenate([a[..., :3072], a[..., C_XW:C_XW + 64], a[..., C_XA:C_XA + 64],
                            a[..., C_XG:C_XG + 160]], axis=-1)


def _layer(x, mod, wkv0, shift0, conv0, pw, per_token, tiles):
    bn, t, _ = x.shape
    ntok = bn * t
    xf = x.reshape(ntok, D)
    chunks = [mod[:, i * D:(i + 1) * D] for i in range(6)]
    if per_token:
        chunks = [jnp.repeat(c, t, axis=0).reshape(1, ntok, D) for c in chunks]
    else:
        chunks = [c.reshape(bn, 1, D) for c in chunks]
    sh_m, sc_m, gt_m, sh_f, sc_f, gt_f = chunks

    proj = _proj_call(xf, sc_m, sh_m, pw["mix_pre_g"], pw["w_in"], per_token, tiles["proj"], t)
    proj = proj.reshape(bn, t, NIN)
    shift_out = _unpad_rwkv_cols(proj[:, t - 1, :NRW])

    shift0_p = _pad_rwkv_cols(shift0).reshape(bn, 1, NRW)
    al, be, k2, r, w, v, g, bo, sga, yb, conv_out = _prep_call(
        proj, shift0_p, conv0, pw, bn, t, tiles["prep"])

    flat = lambda a: a.reshape(ntok, D)
    vh = v.reshape(bn, t, H, N)
    yh, wkv_out = _wkv_call(al, be, k2, r, w, vh, wkv0, bn, t, tiles["wkv"])
    y = yh.reshape(ntok, D)

    x1, hf, gates = _post_call(y, flat(bo), flat(g), flat(sga), flat(yb), xf, gt_m, sc_f, sh_f,
                               pw, per_token, tiles["post"], t)
    x2 = _moe_call(hf, gates, x1, gt_f, pw, per_token, tiles["moe"], t)
    return x2.reshape(bn, t, D), wkv_out, shift_out, conv_out


def kernel(x_prompt, x_sample, state_wkv, state_shift, state_conv, c_prompt, c_sample, w_ada, b_ada, mix_pre_g, mix_post_g, w_in, mu_shift, w0, w_decay2, a0, w_aaa2, w_gate2, k_k, k_a, r_k, gn_g, gn_b, w_o_rwkv, dw_w, dw_b, conv_ln_g, conv_ln_b, w_conv_out, w_out, ffn_pre_g, ffn_post_g, w_router, router_bias, w_exp_gate, w_exp_up, w_exp_down, w_sh_gate, w_sh_up, w_sh_down):
    bp = x_prompt.shape[0]
    bs = x_sample.shape[0]
    row = lambda a: a.reshape(1, -1).astype(F32)
    padr = lambda a, n: jnp.pad(a, ((0, n - a.shape[0]), (0, 0)))
    padc = lambda a, n: jnp.pad(a, ((0, 0), (0, n - a.shape[1])))

    w_in_p = jnp.concatenate(
        [w_in[:, :3072], padc(w_in[:, 3072:3136], 128), padc(w_in[:, 3136:3200], 128),
         padc(w_in[:, 3200:3360], 256), w_in[:, 3360:]], axis=1).astype(BF16)
    head_of_col = jnp.arange(D, dtype=jnp.int32) // N
    seg = (head_of_col[:, None] == jnp.arange(LANES, dtype=jnp.int32)[None, :]).astype(BF16)
    wr = padc(w_router, LANES)
    wr_hi = wr.astype(BF16)
    pw = {
        "w_in": w_in_p,
        "mix_pre_g": row(mix_pre_g), "mix_post_g": row(mix_post_g),
        "ffn_pre_g": row(ffn_pre_g), "ffn_post_g": row(ffn_post_g),
        "mu": row(_pad_rwkv_cols(mu_shift)), "w0": row(w0), "a0": row(a0),
        "k_k": row(k_k), "k_a": row(k_a), "r_k": row(r_k),
        "wd2": padr(w_decay2, 128).astype(BF16), "wa2": padr(w_aaa2, 128).astype(BF16),
        "wg2": padr(w_gate2, 256).astype(BF16),
        "dw_w": dw_w, "dw_b": row(dw_b), "ln_g": row(conv_ln_g), "ln_b": row(conv_ln_b),
        "w_conv_out": w_conv_out.astype(BF16),
        "gn_g": row(gn_g), "gn_b": row(gn_b),
        "w_o_rwkv": w_o_rwkv.astype(BF16), "w_out": w_out.astype(BF16),
        "wr_hi": wr_hi, "wr_lo": (wr - wr_hi.astype(F32)).astype(BF16),
        "router_bias": padc(row(router_bias), LANES),
        "seg": seg, "exp": seg.T,
        "w_sh_gate": w_sh_gate.astype(BF16), "w_sh_up": w_sh_up.astype(BF16),
        "w_sh_down": w_sh_down.astype(BF16),
        "w_exp_gate": w_exp_gate.astype(BF16), "w_exp_up": w_exp_up.astype(BF16),
        "w_exp_down": w_exp_down.astype(BF16),
    }

    mod = _mod_call(jnp.concatenate([c_prompt, c_sample], axis=0), w_ada, b_ada)
    zeros = lambda shape: jnp.zeros(shape, x_prompt.dtype)
    yp, wkv_p, shift_p, conv_p = _layer(
        x_prompt, mod[:bp], None, zeros((bp, state_shift.shape[1])),
        zeros((bp, CONV_W - 1, D)), pw, per_token=False,
        tiles={"proj": 1024, "prep": 256, "wkv": 64, "post": 256, "moe": 1024})
    ts = x_sample.shape[1]
    ys, wkv_s, shift_s, conv_s = _layer(
        x_sample, mod[bp:], state_wkv, state_shift, state_conv, pw, per_token=True,
        tiles={"proj": bs * ts, "prep": ts, "wkv": ts, "post": 256, "moe": bs * ts})
    return (yp, ys, wkv_p, shift_p, conv_p, wkv_s, shift_s, conv_s)
```

```python
import functools

import jax
import jax.numpy as jnp
from jax import lax
from jax.experimental import pallas as pl
from jax.experimental.pallas import tpu as pltpu

F32 = jnp.float32
BF16 = jnp.bfloat16

D = 1024
H = 16
N = 64
E = 64
DE = 256
TOP_K = 6
ROUTED_SCALE = 2.5
CONV_W = 31
RMS_EPS = 1e-6
LN_EPS = 1e-5
GN_EPS = N * 1e-5

C_R, C_K, C_V = 0, 1024, 2048
C_XW, C_XA, C_XG = 3072, 3200, 3328
NRW = 3584
C_CVA, C_CVB, C_GA, C_GB = 3584, 4608, 5632, 6656
NIN = 7680

LANES = 128
HG = 4
GL = HG * N
NG = H // HG
WKV_SEQS = 8
WKV_UNROLL = 4
VMEM_LIMIT = 56 * 1024 * 1024


def _dot(a, b):
    return jnp.dot(a, b, preferred_element_type=F32)


def _sigmoid(x):
    return jax.nn.sigmoid(x)


def _split_bf16(x):
    hi = x.astype(BF16)
    lo = (x - hi.astype(F32)).astype(BF16)
    return hi, lo


def _head_sum(x, seg, exp):
    hi, lo = _split_bf16(x)
    s = _dot(hi, seg) + _dot(lo, seg)
    shi, slo = _split_bf16(s)
    return _dot(shi, exp) + _dot(slo, exp)


def _params(sem, vmem=VMEM_LIMIT):
    return pltpu.CompilerParams(dimension_semantics=sem, vmem_limit_bytes=vmem)


def _mod_body(c_ref, w_ref, b_ref, o_ref):
    c = c_ref[...]
    s = (c * _sigmoid(c)).astype(BF16)
    o_ref[...] = _dot(s, w_ref[...].astype(BF16)) + b_ref[...]


def _mod_call(c_all, w_ada, b_ada):
    rows = c_all.shape[0]
    tn = 512
    return pl.pallas_call(
        _mod_body,
        grid=(6 * D // tn,),
        in_specs=[pl.BlockSpec((rows, D), lambda j: (0, 0)),
                  pl.BlockSpec((D, tn), lambda j: (0, j)),
                  pl.BlockSpec((1, tn), lambda j: (0, j))],
        out_specs=pl.BlockSpec((rows, tn), lambda j: (0, j)),
        out_shape=jax.ShapeDtypeStruct((rows, 6 * D), F32),
        compiler_params=_params(("arbitrary",)),
        name="ada_mod",
    )(c_all, w_ada, b_ada.reshape(1, 6 * D))


def _mod_spec(per_token, tm, rows_per_seq):
    if per_token:
        return pl.BlockSpec((None, tm, D), lambda i, *_: (0, i, 0))
    return pl.BlockSpec((None, 1, D), lambda i, *_: (i * tm // rows_per_seq, 0, 0))


def _proj_body(x_ref, sc_ref, sh_ref, g_ref, w_ref, o_ref, h_scr):
    @pl.when(pl.program_id(1) == 0)
    def _():
        x = x_ref[...]
        y = x * lax.rsqrt(jnp.mean(x * x, axis=-1, keepdims=True) + RMS_EPS) * g_ref[...]
        h_scr[...] = (y * (1.0 + sc_ref[...]) + sh_ref[...]).astype(BF16)

    o_ref[...] = _dot(h_scr[...], w_ref[...])


def _proj_call(x, sc, sh, g, w_in_p, per_token, tm, rows_per_seq):
    ntok = x.shape[0]
    tn = 1280
    mspec = _mod_spec(per_token, tm, rows_per_seq)
    return pl.pallas_call(
        _proj_body,
        grid=(ntok // tm, NIN // tn),
        in_specs=[pl.BlockSpec((tm, D), lambda i, j: (i, 0)),
                  mspec, mspec,
                  pl.BlockSpec((1, D), lambda i, j: (0, 0)),
                  pl.BlockSpec((D, tn), lambda i, j: (0, j))],
        out_specs=pl.BlockSpec((tm, tn), lambda i, j: (i, j)),
        out_shape=jax.ShapeDtypeStruct((ntok, NIN), F32),
        scratch_shapes=[pltpu.VMEM((tm, D), BF16)],
        compiler_params=_params(("arbitrary", "arbitrary")),
        name="in_proj",
    )(x, sc, sh, g, w_in_p)


def _prep_body(proj_ref, sh0_ref, cv0_ref, mu_ref, w0_ref, a0_ref, kk_ref, ka_ref, rk_ref,
               wd2_ref, wa2_ref, wg2_ref, dww_ref, dwb_ref, lng_ref, lnb_ref, wco_ref,
               seg_ref, exp_ref,
               al_o, be_o, k_o, r_o, w_o, v_o, g_o, bo_o, sga_o, yb_o, cv_o,
               buf_scr, ext_scr, part_scr, *, tm):
    ti = pl.program_id(1)

    @pl.when(ti == 0)
    def _():
        buf_scr[7:8, :] = sh0_ref[...]
        ext_scr[0:2, :] = jnp.zeros((2, D), F32)
        ext_scr[2:32, :] = cv0_ref[...]

    p = proj_ref[:, 0:NRW]
    buf_scr[8:8 + tm, :] = p
    prev = buf_scr[7:7 + tm, :]
    xs = p + (prev - p) * mu_ref[...]
    buf_scr[7:8, :] = p[tm - 1:tm, :]

    r = xs[:, C_R:C_R + D]
    k = xs[:, C_K:C_K + D]
    v = xs[:, C_V:C_V + D]
    xw = xs[:, C_XW:C_XW + 128]
    xa = xs[:, C_XA:C_XA + 128]
    xg = xs[:, C_XG:C_XG + 256]

    lw = _dot(jnp.tanh(xw).astype(BF16), wd2_ref[...])
    z = -(w0_ref[...] + lw)
    softplus = jnp.maximum(z, 0.0) + jnp.log(1.0 + jnp.exp(-jnp.abs(z)))
    decay = jnp.exp(-jnp.exp(-softplus - 0.5))
    a = _sigmoid(a0_ref[...] + _dot(xa.astype(BF16), wa2_ref[...]))
    g_o[...] = _dot(_sigmoid(xg).astype(BF16), wg2_ref[...])

    seg = seg_ref[...]
    exp = exp_ref[...]
    kk = k * kk_ref[...]
    kk = kk / jnp.maximum(jnp.sqrt(_head_sum(kk * kk, seg, exp)), 1e-12)
    k2 = k * (1.0 + (a - 1.0) * ka_ref[...])
    bo_o[...] = _head_sum(r * k2 * rk_ref[...], seg, exp) * v
    al_o[...] = -kk
    be_o[...] = kk * a
    k_o[...] = k2
    r_o[...] = r
    w_o[...] = decay
    v_o[...] = v.astype(BF16)

    glu = proj_ref[:, C_CVA:C_CVA + D] * _sigmoid(proj_ref[:, C_CVB:C_CVB + D])
    ext_scr[32:32 + tm, :] = glu
    acc = jnp.zeros((tm, D), F32) + dwb_ref[...]
    for s in range(8):
        rows = tm if s == 0 else tm + 8
        part = None
        for q in range(5):
            o = 8 * q + s
            if 2 <= o <= CONV_W + 1:
                term = ext_scr[8 * q:8 * q + rows, :] * dww_ref[o - 2:o - 1, :]
                part = term if part is None else part + term
        if s == 0:
            acc = acc + part
        else:
            part_scr[...] = part
            acc = acc + part_scr[s:s + tm, :]
    tail = ext_scr[tm:tm + 32, :]
    cv_o[...] = tail[2:32, :]
    ext_scr[0:32, :] = tail
    mean = jnp.mean(acc, axis=-1, keepdims=True)
    cen = acc - mean
    var = jnp.mean(cen * cen, axis=-1, keepdims=True)
    ln = cen * lax.rsqrt(var + LN_EPS) * lng_ref[...] + lnb_ref[...]
    u = (ln * _sigmoid(ln)).astype(BF16)
    yb = _dot(u, wco_ref[...])
    sga_o[...] = _sigmoid(proj_ref[:, C_GA:C_GA + D])
    yb_o[...] = _sigmoid(proj_ref[:, C_GB:C_GB + D]) * yb


def _prep_call(proj, shift0_p, conv0, pw, bn, t, tm):
    nt = t // tm
    row = lambda shape: pl.BlockSpec(shape, lambda b, i: (0,) * len(shape))
    tok = pl.BlockSpec((None, tm, D), lambda b, i: (b, i, 0))
    ins = [pl.BlockSpec((None, tm, NIN), lambda b, i: (b, i, 0)),
           pl.BlockSpec((None, 1, NRW), lambda b, i: (b, 0, 0)),
           pl.BlockSpec((None, CONV_W - 1, D), lambda b, i: (b, 0, 0)),
           row((1, NRW)), row((1, D)), row((1, D)), row((1, D)), row((1, D)), row((1, D)),
           row((128, D)), row((128, D)), row((256, D)),
           row((CONV_W, D)), row((1, D)), row((1, D)), row((1, D)), row((D, D)),
           row((D, LANES)), row((LANES, D))]
    outs = [tok] * 10 + [pl.BlockSpec((None, CONV_W - 1, D), lambda b, i: (b, 0, 0))]
    out_dtypes = [F32] * 5 + [BF16] + [F32] * 4
    out_shape = [jax.ShapeDtypeStruct((bn, t, D), dt) for dt in out_dtypes] + [
        jax.ShapeDtypeStruct((bn, CONV_W - 1, D), F32)]
    return pl.pallas_call(
        functools.partial(_prep_body, tm=tm),
        grid=(bn, nt),
        in_specs=ins,
        out_specs=outs,
        out_shape=out_shape,
        scratch_shapes=[pltpu.VMEM((tm + 8, NRW), F32), pltpu.VMEM((tm + 32, D), F32),
                        pltpu.VMEM((tm + 8, D), F32)],
        compiler_params=_params(("arbitrary", "arbitrary")),
        name="rwkv_prep_conv",
    )(proj, shift0_p, conv0, pw["mu"], pw["w0"], pw["a0"], pw["k_k"], pw["k_a"], pw["r_k"],
      pw["wd2"], pw["wa2"], pw["wg2"], pw["dw_w"], pw["dw_b"], pw["ln_g"], pw["ln_b"],
      pw["w_conv_out"], pw["seg"], pw["exp"])


def _wkv_body(al_ref, be_ref, k_ref, r_ref, w_ref, vh_ref, *rest, tb, zero_init):
    s0_ref = None if zero_init else rest[0]
    y_ref, so_ref, s_scr, mv_scr = rest[-4:]
    ti = pl.program_id(1)

    @pl.when(ti == 0)
    def _():
        if zero_init:
            s_scr[...] = jnp.zeros(s_scr.shape, F32)
        else:
            for b in range(WKV_SEQS):
                for g in range(NG):
                    s_scr[b, g] = jnp.concatenate(
                        [s0_ref[b, HG * g + h] for h in range(HG)], axis=-1)

    row_x = lax.broadcasted_iota(jnp.int32, (8, GL), 0)
    lane_x = lax.broadcasted_iota(jnp.int32, (8, GL), 1)
    head_mask = (lane_x // N == row_x % HG).astype(F32)
    low_x = row_x < HG
    low_o = lax.broadcasted_iota(jnp.int32, (8, N), 0) < HG

    def matvec(b, g, t_a, t_r):
        lanes = pl.ds(g * GL, GL)
        a_b = jnp.broadcast_to(al_ref[b, pl.ds(t_a, 1), lanes], (8, GL))
        r_b = jnp.broadcast_to(r_ref[b, pl.ds(t_r, 1), lanes], (8, GL))
        x = (jnp.where(low_x, r_b, a_b) if g % 2 == 0 else jnp.where(low_x, a_b, r_b)) * head_mask
        s = s_scr[b, g]
        return lax.dot_general(x.astype(BF16), s.astype(BF16), (((1,), (1,)), ((), ())),
                               preferred_element_type=F32)

    groups = [(b, g) for b in range(WKV_SEQS) for g in range(NG)]
    n_grp = len(groups)
    lag = n_grp // 2

    def matvec_y(i, t_a, t_r):
        b, g = groups[i]
        out = matvec(b, g, t_a, t_r)
        mv_scr[b, g] = out
        if g % 2 == 1:
            j = g // 2
            y_ref[b, t_r, 8 * j:8 * j + 8, :] = jnp.where(low_o, mv_scr[b, g - 1], out)

    def update(i, t):
        b, g = groups[i]
        lanes = pl.ds(g * GL, GL)
        j = g // 2
        vt = vh_ref[b, t, 8 * j:8 * j + 8, :].astype(F32)
        out = mv_scr[b, g]
        lt = jnp.where(low_o, vt, out) if g % 2 == 0 else jnp.where(low_o, out, vt)
        k_b = jnp.broadcast_to(k_ref[b, pl.ds(t, 1), lanes], (8, GL))
        be_b = jnp.broadcast_to(be_ref[b, pl.ds(t, 1), lanes], (8, GL))
        q = (jnp.where(low_x, k_b, be_b) if g % 2 == 0
             else jnp.where(low_x, be_b, k_b)) * head_mask
        ds = lax.dot_general(lt.astype(BF16), q.astype(BF16), (((0,), (0,)), ((), ())),
                             preferred_element_type=F32)
        s_scr[b, g] = s_scr[b, g] * w_ref[b, pl.ds(t, 1), lanes] + ds

    for i in range(n_grp):
        matvec_y(i, 0, 0)
        if i >= lag:
            update(i - lag, 0)

    def step(t, carry):
        for i in range(n_grp):
            matvec_y(i, t, t - 1)
            if i < lag:
                update(i + lag, t - 1)
            else:
                update(i - lag, t)
        return carry

    lax.fori_loop(1, tb, step, 0, unroll=WKV_UNROLL)

    for i in range(lag, n_grp):
        update(i, tb - 1)
    for i in range(n_grp):
        matvec_y(i, tb - 1, tb - 1)

    @pl.when(ti == pl.num_programs(1) - 1)
    def _():
        for b in range(WKV_SEQS):
            for g in range(NG):
                s = s_scr[b, g]
                for h in range(HG):
                    so_ref[b, HG * g + h] = s[:, N * h:N * (h + 1)]


def _wkv_call(al, be, k2, r, w, vh, s0, bn, t, tb):
    seq = pl.BlockSpec((WKV_SEQS, tb, D), lambda i, j: (i, j, 0))
    hv = pl.BlockSpec((WKV_SEQS, tb, H, N), lambda i, j: (i, j, 0, 0))
    st = pl.BlockSpec((WKV_SEQS, H, N, N), lambda i, j: (i, 0, 0, 0))
    zero_init = s0 is None
    return pl.pallas_call(
        functools.partial(_wkv_body, tb=tb, zero_init=zero_init),
        grid=(bn // WKV_SEQS, t // tb),
        in_specs=[seq, seq, seq, seq, seq, hv] + ([] if zero_init else [st]),
        out_specs=[hv, st],
        out_shape=[jax.ShapeDtypeStruct((bn, t, H, N), F32),
                   jax.ShapeDtypeStruct((bn, H, N, N), F32)],
        scratch_shapes=[pltpu.VMEM((WKV_SEQS, NG, N, GL), F32),
                        pltpu.VMEM((WKV_SEQS, NG, 8, N), F32)],
        compiler_params=_params(("arbitrary", "arbitrary")),
        name="wkv7_scan",
    )(*((al, be, k2, r, w, vh) + (() if zero_init else (s0,))))


def _post_body(y_ref, bo_ref, g_ref, sga_ref, yb_ref, x_ref, gtm_ref, scf_ref, shf_ref,
               gng_ref, gnb_ref, wo_ref, wout_ref, postg_ref, preg_ref,
               wrh_ref, wrl_ref, rb_ref, seg_ref, exp_ref,
               x1_o, hf_o, gates_o):
    seg = seg_ref[...]
    exp = exp_ref[...]
    y = y_ref[...]
    mu = _head_sum(y, seg, exp) * (1.0 / N)
    cen = y - mu
    var = _head_sum(cen * cen, seg, exp) * (1.0 / N)
    yn = cen * lax.rsqrt(var + GN_EPS) * gng_ref[...] + gnb_ref[...] + bo_ref[...]
    ya = _dot((yn * g_ref[...]).astype(BF16), wo_ref[...])
    merged = sga_ref[...] * ya + yb_ref[...]
    z = _dot(merged.astype(BF16), wout_ref[...])
    zn = z * lax.rsqrt(jnp.mean(z * z, axis=-1, keepdims=True) + RMS_EPS) * postg_ref[...]
    x1 = x_ref[...] + gtm_ref[...] * zn
    x1_o[...] = x1
    hn = x1 * lax.rsqrt(jnp.mean(x1 * x1, axis=-1, keepdims=True) + RMS_EPS) * preg_ref[...]
    hf = hn * (1.0 + scf_ref[...]) + shf_ref[...]
    hf_o[...] = hf.astype(BF16)

    hh, hl = _split_bf16(hf)
    logits = _dot(hh, wrh_ref[...]) + _dot(hl, wrh_ref[...]) + _dot(hh, wrl_ref[...])
    scores = _sigmoid(logits)
    lane = lax.broadcasted_iota(jnp.int32, scores.shape, 1).astype(F32)
    neg = jnp.float32(-jnp.inf)
    sb = jnp.where(lane < E, scores + rb_ref[...], neg)
    picked = jnp.zeros(scores.shape, F32)
    for _ in range(TOP_K):
        m = jnp.max(sb, axis=-1, keepdims=True)
        first = jnp.min(jnp.where(sb == m, lane, float(LANES)), axis=-1, keepdims=True)
        pick = lane == first
        picked = jnp.where(pick, 1.0, picked)
        sb = jnp.where(pick, neg, sb)
    sel = scores * picked
    gates_o[...] = sel / jnp.sum(sel, axis=-1, keepdims=True) * ROUTED_SCALE


def _post_call(y, bo, g, sga, yb, x, gtm, scf, shf, pw, per_token, tm, rows_per_seq):
    ntok = x.shape[0]
    tok = pl.BlockSpec((tm, D), lambda i: (i, 0))
    row = lambda shape: pl.BlockSpec(shape, lambda i: (0,) * len(shape))
    mspec = _mod_spec(per_token, tm, rows_per_seq)
    return pl.pallas_call(
        _post_body,
        grid=(ntok // tm,),
        in_specs=[tok] * 6 + [mspec] * 3 + [
            row((1, D)), row((1, D)), row((D, D)), row((D, D)), row((1, D)), row((1, D)),
            row((D, LANES)), row((D, LANES)), row((1, LANES)), row((D, LANES)), row((LANES, D))],
        out_specs=[tok, tok, pl.BlockSpec((tm, LANES), lambda i: (i, 0))],
        out_shape=[jax.ShapeDtypeStruct((ntok, D), F32),
                   jax.ShapeDtypeStruct((ntok, D), BF16),
                   jax.ShapeDtypeStruct((ntok, LANES), F32)],
        compiler_params=_params(("arbitrary",)),
        name="mix_post_router",
    )(y, bo, g, sga, yb, x, gtm, scf, shf, pw["gn_g"], pw["gn_b"], pw["w_o_rwkv"], pw["w_out"],
      pw["mix_post_g"], pw["ffn_pre_g"], pw["wr_hi"], pw["wr_lo"], pw["router_bias"],
      pw["seg"], pw["exp"])


MOE_EB = 4


def _moe_body(hf_ref, gates_ref, x1_ref, gtf_ref, postg_ref, wsg_ref, wsu_ref, wsd_ref,
              wg_ref, wu_ref, wd_ref, o_ref, acc_scr):
    e0 = pl.program_id(1)
    t = hf_ref[...]

    @pl.when(e0 == 0)
    def _():
        hg = _dot(t, wsg_ref[...])
        he = hg * _sigmoid(hg) * _dot(t, wsu_ref[...])
        acc_scr[...] = _dot(he.astype(BF16), wsd_ref[...])

    gates = gates_ref[...]
    lane = lax.broadcasted_iota(jnp.int32, gates.shape, 1)
    hidden = []
    for i in range(MOE_EB):
        gcol = jnp.sum(jnp.where(lane == e0 * MOE_EB + i, gates, 0.0), axis=-1, keepdims=True)
        hg = _dot(t, wg_ref[i])
        he = hg * _sigmoid(hg) * _dot(t, wu_ref[i])
        hidden.append((he * gcol).astype(BF16))
    wd = wd_ref[...].reshape(MOE_EB * DE, D)
    acc_scr[...] += _dot(jnp.concatenate(hidden, axis=-1), wd)

    @pl.when(e0 == pl.num_programs(1) - 1)
    def _():
        out = acc_scr[...]
        on = out * lax.rsqrt(jnp.mean(out * out, axis=-1, keepdims=True) + RMS_EPS) * postg_ref[...]
        o_ref[...] = x1_ref[...] + gtf_ref[...] * on


def _moe_call(hf, gates, x1, gtf, pw, per_token, tm, rows_per_seq):
    ntok = hf.shape[0]
    tok = pl.BlockSpec((tm, D), lambda i, e: (i, 0))
    row = lambda shape: pl.BlockSpec(shape, lambda i, e: (0,) * len(shape))
    mspec = _mod_spec(per_token, tm, rows_per_seq)
    return pl.pallas_call(
        _moe_body,
        grid=(ntok // tm, E // MOE_EB),
        in_specs=[tok, pl.BlockSpec((tm, LANES), lambda i, e: (i, 0)), tok, mspec,
                  row((1, D)), row((D, DE)), row((D, DE)), row((DE, D)),
                  pl.BlockSpec((MOE_EB, D, DE), lambda i, e: (e, 0, 0)),
                  pl.BlockSpec((MOE_EB, D, DE), lambda i, e: (e, 0, 0)),
                  pl.BlockSpec((MOE_EB, DE, D), lambda i, e: (e, 0, 0))],
        out_specs=tok,
        out_shape=jax.ShapeDtypeStruct((ntok, D), F32),
        scratch_shapes=[pltpu.VMEM((tm, D), F32)],
        compiler_params=_params(("arbitrary", "arbitrary")),
        name="moe_ffn",
    )(hf, gates, x1, gtf, pw["ffn_post_g"], pw["w_sh_gate"], pw["w_sh_up"], pw["w_sh_down"],
      pw["w_exp_gate"], pw["w_exp_up"], pw["w_exp_down"])


def _pad_rwkv_cols(a):
    pad = lambda z, n: jnp.pad(z, [(0, 0)] * (z.ndim - 1) + [(0, n - z.shape[-1])])
    return jnp.concatenate([a[..., :3072], pad(a[..., 3072:3136], 128), pad(a[..., 3136:3200], 128),
                            pad(a[..., 3200:3360], 256)], axis=-1)


def _unpad_rwkv_cols(a):
    return jnp.concatenate([a[..., :3072], a[..., C_XW:C_XW + 64], a[..., C_XA:C_XA + 64],
                            a[..., C_XG:C_XG + 160]], axis=-1)


def _layer(x, mod, wkv0, shift0, conv0, pw, per_token, tiles):
    bn, t, _ = x.shape
    ntok = bn * t
    xf = x.reshape(ntok, D)
    chunks = [mod[:, i * D:(i + 1) * D] for i in range(6)]
    if per_token:
        chunks = [jnp.repeat(c, t, axis=0).reshape(1, ntok, D) for c in chunks]
    else:
        chunks = [c.reshape(bn, 1, D) for c in chunks]
    sh_m, sc_m, gt_m, sh_f, sc_f, gt_f = chunks

    proj = _proj_call(xf, sc_m, sh_m, pw["mix_pre_g"], pw["w_in"], per_token, tiles["proj"], t)
    proj = proj.reshape(bn, t, NIN)
    shift_out = _unpad_rwkv_cols(proj[:, t - 1, :NRW])

    shift0_p = _pad_rwkv_cols(shift0).reshape(bn, 1, NRW)
    al, be, k2, r, w, v, g, bo, sga, yb, conv_out = _prep_call(
        proj, shift0_p, conv0, pw, bn, t, tiles["prep"])

    flat = lambda a: a.reshape(ntok, D)
    vh = v.reshape(bn, t, H, N)
    yh, wkv_out = _wkv_call(al, be, k2, r, w, vh, wkv0, bn, t, tiles["wkv"])
    y = yh.reshape(ntok, D)

    x1, hf, gates = _post_call(y, flat(bo), flat(g), flat(sga), flat(yb), xf, gt_m, sc_f, sh_f,
                               pw, per_token, tiles["post"], t)
    x2 = _moe_call(hf, gates, x1, gt_f, pw, per_token, tiles["moe"], t)
    return x2.reshape(bn, t, D), wkv_out, shift_out, conv_out


def kernel(x_prompt, x_sample, state_wkv, state_shift, state_conv, c_prompt, c_sample, w_ada, b_ada, mix_pre_g, mix_post_g, w_in, mu_shift, w0, w_decay2, a0, w_aaa2, w_gate2, k_k, k_a, r_k, gn_g, gn_b, w_o_rwkv, dw_w, dw_b, conv_ln_g, conv_ln_b, w_conv_out, w_out, ffn_pre_g, ffn_post_g, w_router, router_bias, w_exp_gate, w_exp_up, w_exp_down, w_sh_gate, w_sh_up, w_sh_down):
    bp = x_prompt.shape[0]
    bs = x_sample.shape[0]
    row = lambda a: a.reshape(1, -1).astype(F32)
    padr = lambda a, n: jnp.pad(a, ((0, n - a.shape[0]), (0, 0)))
    padc = lambda a, n: jnp.pad(a, ((0, 0), (0, n - a.shape[1])))

    w_in_p = jnp.concatenate(
        [w_in[:, :3072], padc(w_in[:, 3072:3136], 128), padc(w_in[:, 3136:3200], 128),
         padc(w_in[:, 3200:3360], 256), w_in[:, 3360:]], axis=1).astype(BF16)
    head_of_col = jnp.arange(D, dtype=jnp.int32) // N
    seg = (head_of_col[:, None] == jnp.arange(LANES, dtype=jnp.int32)[None, :]).astype(BF16)
    wr = padc(w_router, LANES)
    wr_hi = wr.astype(BF16)
    pw = {
        "w_in": w_in_p,
        "mix_pre_g": row(mix_pre_g), "mix_post_g": row(mix_post_g),
        "ffn_pre_g": row(ffn_pre_g), "ffn_post_g": row(ffn_post_g),
        "mu": row(_pad_rwkv_cols(mu_shift)), "w0": row(w0), "a0": row(a0),
        "k_k": row(k_k), "k_a": row(k_a), "r_k": row(r_k),
        "wd2": padr(w_decay2, 128).astype(BF16), "wa2": padr(w_aaa2, 128).astype(BF16),
        "wg2": padr(w_gate2, 256).astype(BF16),
        "dw_w": dw_w, "dw_b": row(dw_b), "ln_g": row(conv_ln_g), "ln_b": row(conv_ln_b),
        "w_conv_out": w_conv_out.astype(BF16),
        "gn_g": row(gn_g), "gn_b": row(gn_b),
        "w_o_rwkv": w_o_rwkv.astype(BF16), "w_out": w_out.astype(BF16),
        "wr_hi": wr_hi, "wr_lo": (wr - wr_hi.astype(F32)).astype(BF16),
        "router_bias": padc(row(router_bias), LANES),
        "seg": seg, "exp": seg.T,
        "w_sh_gate": w_sh_gate.astype(BF16), "w_sh_up": w_sh_up.astype(BF16),
        "w_sh_down": w_sh_down.astype(BF16),
        "w_exp_gate": w_exp_gate.astype(BF16), "w_exp_up": w_exp_up.astype(BF16),
        "w_exp_down": w_exp_down.astype(BF16),
    }

    mod = _mod_call(jnp.concatenate([c_prompt, c_sample], axis=0), w_ada, b_ada)
    zeros = lambda shape: jnp.zeros(shape, x_prompt.dtype)
    yp, wkv_p, shift_p, conv_p = _layer(
        x_prompt, mod[:bp], None, zeros((bp, state_shift.shape[1])),
        zeros((bp, CONV_W - 1, D)), pw, per_token=False,
        tiles={"proj": 1024, "prep": 256, "wkv": 64, "post": 256, "moe": 1024})
    ts = x_sample.shape[1]
    ys, wkv_s, shift_s, conv_s = _layer(
        x_sample, mod[bp:], state_wkv, state_shift, state_conv, pw, per_token=True,
        tiles={"proj": bs * ts, "prep": ts, "wkv": ts, "post": 256, "moe": bs * ts})
    return (yp, ys, wkv_p, shift_p, conv_p, wkv_s, shift_s, conv_s)
```
